```python
import jax, jax.numpy as jnp
from jax import lax
import numpy as np

D_MODEL = 1024
BATCH = 2
SEQ = 8192
DEPTH = 2

N_MIXERS = 2
N_A_LAYERS = (DEPTH + 1) // 2
N_B_LAYERS = DEPTH // 2
EPS = 1e-6

D_FF = ((8 * D_MODEL // 3 + 127) // 128) * 128

A_DK = 128
A_DV = 128
A_HEADS = D_MODEL // A_DK
A_CONV = 4
A_CHUNK = 64

B_HD = 64
B_HEADS = D_MODEL // B_HD
B_KV_HEADS = 4
B_WINDOW = 128
B_BLOCK = 128

kernel_name = "hybrid_gdn_swa_sink_macaron"


def rmsnorm(x, w):
    xf = x.astype(jnp.float32)
    y = xf * lax.rsqrt(jnp.mean(xf * xf, axis=-1, keepdims=True) + EPS) * w.astype(jnp.float32)
    return y.astype(x.dtype)


def l2norm(x):
    xf = x.astype(jnp.float32)
    return xf * lax.rsqrt(jnp.sum(xf * xf, axis=-1, keepdims=True) + EPS)


def swiglu(h, w_gu, w_down):
    gate, up = jnp.split(h @ w_gu, 2, axis=-1)
    return (jax.nn.silu(gate) * up) @ w_down


def causal_depthwise_conv(x, w):
    C = x.shape[-1]
    return lax.conv_general_dilated(
        x, w[:, None, :].astype(x.dtype), window_strides=(1,), padding=[(A_CONV - 1, 0)],
        dimension_numbers=("NWC", "WIO", "NWC"), feature_group_count=C)


def gated_delta_rule_chunked(q, k, v, g, beta):
    Bsz, T, H, DK = q.shape
    DV = v.shape[-1]
    C = A_CHUNK
    N = T // C
    f32 = jnp.float32

    def chunk(t):
        return t.astype(f32).reshape(Bsz, N, C, H, -1).transpose(0, 3, 1, 2, 4)

    q, k, v = chunk(q), chunk(k), chunk(v)
    g = g.astype(f32).reshape(Bsz, N, C, H).transpose(0, 3, 1, 2)
    beta = beta.astype(f32).reshape(Bsz, N, C, H).transpose(0, 3, 1, 2)
    g = jnp.cumsum(g, axis=-1)

    idx = jnp.arange(C)
    lower_incl = idx[:, None] >= idx[None, :]
    strict = idx[:, None] > idx[None, :]
    decay = jnp.exp(jnp.where(lower_incl, g[..., :, None] - g[..., None, :], -jnp.inf))

    kb = k * beta[..., None]
    L = jnp.where(strict, jnp.einsum("bhnid,bhnjd->bhnij", kb, k) * decay, 0.0)
    rhs = jnp.concatenate([v * beta[..., None], kb * jnp.exp(g)[..., None]], axis=-1)
    sol = lax.linalg.triangular_solve(L, rhs, left_side=True, lower=True,
                                      transpose_a=False, conjugate_a=False, unit_diagonal=True)
    u, w = sol[..., :DV], sol[..., DV:]

    a_qk = jnp.einsum("bhnid,bhnjd->bhnij", q, k) * decay
    q_dec = q * jnp.exp(g)[..., None]
    k_dec = k * jnp.exp(g[..., -1:] - g)[..., None]
    g_last = jnp.exp(g[..., -1])

    xs = (jnp.moveaxis(q_dec, 2, 0), jnp.moveaxis(k_dec, 2, 0), jnp.moveaxis(u, 2, 0),
          jnp.moveaxis(w, 2, 0), jnp.moveaxis(a_qk, 2, 0), jnp.moveaxis(g_last, 2, 0))

    def step(S, inp):
        qd, kd, u_c, w_c, a_c, gl = inp
        v_new = u_c - jnp.einsum("bhck,bhkv->bhcv", w_c, S)
        o = jnp.einsum("bhck,bhkv->bhcv", qd, S) + jnp.einsum("bhij,bhjv->bhiv", a_c, v_new)
        S = S * gl[..., None, None] + jnp.einsum("bhck,bhcv->bhkv", kd, v_new)
        return S, o

    S0 = jnp.zeros((Bsz, H, DK, DV), f32)
    _, o = lax.scan(step, S0, xs)
    return o.transpose(1, 0, 3, 2, 4).reshape(Bsz, T, H, DV)


def mixer_gated_deltanet(h, w_in, w_conv, A_log, dt_bias, out_norm, w_out):
    Bsz, T, _ = h.shape
    HK = A_HEADS * A_DK
    HV = A_HEADS * A_DV
    proj = h @ w_in
    qkv = proj[..., :2 * HK + HV]
    z = proj[..., 2 * HK + HV:2 * HK + 2 * HV]
    b = proj[..., 2 * HK + 2 * HV:2 * HK + 2 * HV + A_HEADS]
    a = proj[..., 2 * HK + 2 * HV + A_HEADS:]
    qkv = jax.nn.silu(causal_depthwise_conv(qkv, w_conv))
    q = l2norm(qkv[..., :HK].reshape(Bsz, T, A_HEADS, A_DK)) * (A_DK ** -0.5)
    k = l2norm(qkv[..., HK:2 * HK].reshape(Bsz, T, A_HEADS, A_DK))
    v = qkv[..., 2 * HK:].reshape(Bsz, T, A_HEADS, A_DV)
    beta = jax.nn.sigmoid(b.astype(jnp.float32))
    g = -jnp.exp(A_log.astype(jnp.float32)) * jax.nn.softplus(a.astype(jnp.float32) + dt_bias.astype(jnp.float32))
    o = gated_delta_rule_chunked(q, k, v, g, beta)
    zf = z.reshape(Bsz, T, A_HEADS, A_DV).astype(jnp.float32)
    o = rmsnorm(o, out_norm) * jax.nn.silu(zf)
    return o.reshape(Bsz, T, HV).astype(h.dtype) @ w_out


def mixer_sliding_window_sinks(h, w_in, b_in, sinks, w_out, b_out):
    Bsz, T, _ = h.shape
    G = B_HEADS // B_KV_HEADS
    NB = T // B_BLOCK
    HQ = B_HEADS * B_HD
    HKV = B_KV_HEADS * B_HD
    proj = h @ w_in + b_in
    q = proj[..., :HQ].reshape(Bsz, NB, B_BLOCK, B_KV_HEADS, G, B_HD)
    k = proj[..., HQ:HQ + HKV].reshape(Bsz, NB, B_BLOCK, B_KV_HEADS, B_HD)
    v = proj[..., HQ + HKV:].reshape(Bsz, NB, B_BLOCK, B_KV_HEADS, B_HD)

    def with_prev(t):
        prev = jnp.concatenate([jnp.zeros_like(t[:, :1]), t[:, :-1]], axis=1)
        return jnp.concatenate([prev, t], axis=2)

    kk, vv = with_prev(k), with_prev(v)
    s = jnp.einsum("bnqhgd,bnkhd->bnhgqk", q, kk).astype(jnp.float32) * (B_HD ** -0.5)
    qi = jnp.arange(B_BLOCK)[:, None]
    kj = jnp.arange(2 * B_BLOCK)[None, :]
    rel = qi + B_BLOCK - kj
    band = (rel >= 0) & (rel < B_WINDOW)
    blk = jnp.arange(NB)[:, None, None]
    valid = band[None] & ((blk > 0) | (kj >= B_BLOCK)[None])
    s = jnp.where(valid[None, :, None, None], s, -jnp.inf)
    sink = jnp.broadcast_to(sinks.astype(jnp.float32).reshape(B_KV_HEADS, G)[None, None, :, :, None, None],
                            s.shape[:-1] + (1,))
    p = jax.nn.softmax(jnp.concatenate([s, sink], axis=-1), axis=-1)[..., :-1]
    o = jnp.einsum("bnhgqk,bnkhd->bnqhgd", p.astype(vv.dtype), vv)
    return o.reshape(Bsz, T, HQ) @ w_out + b_out


def setup_inputs(seed: int = 0) -> dict:
    key = jax.random.key(seed)
    ks = iter(jax.random.split(key, 32))
    f32 = jnp.float32
    nrm = lambda shape, scale: jax.random.normal(next(ks), shape, f32) * scale
    gain = lambda shape: 1.0 + 0.02 * jax.random.normal(next(ks), shape, f32)
    D = D_MODEL
    a_in_cols = 2 * A_HEADS * A_DK + 2 * A_HEADS * A_DV + 2 * A_HEADS
    b_in_cols = (B_HEADS + 2 * B_KV_HEADS) * B_HD
    dt = jnp.exp(jax.random.uniform(next(ks), (N_A_LAYERS, A_HEADS), f32, np.log(1e-3), np.log(1e-1)))
    return {
        "x": jax.random.normal(next(ks), (BATCH, SEQ, D), f32),
        "ffn1_norm": gain((DEPTH, D)),
        "ffn1_w_gu": nrm((DEPTH, D, 2 * D_FF), D ** -0.5),
        "ffn1_w_down": nrm((DEPTH, D_FF, D), D_FF ** -0.5),
        "mix_norm": gain((DEPTH, D)),
        "ffn2_norm": gain((DEPTH, D)),
        "ffn2_w_gu": nrm((DEPTH, D, 2 * D_FF), D ** -0.5),
        "ffn2_w_down": nrm((DEPTH, D_FF, D), D_FF ** -0.5),
        "a_w_in": nrm((N_A_LAYERS, D, a_in_cols), D ** -0.5),
        "a_w_conv": nrm((N_A_LAYERS, A_CONV, 2 * A_HEADS * A_DK + A_HEADS * A_DV), A_CONV ** -0.5),
        "a_A_log": jnp.log(jax.random.uniform(next(ks), (N_A_LAYERS, A_HEADS), f32, 1.0, 16.0)),
        "a_dt_bias": dt + jnp.log(-jnp.expm1(-dt)),
        "a_out_norm": gain((N_A_LAYERS, A_DV)),
        "a_w_out": nrm((N_A_LAYERS, A_HEADS * A_DV, D), (A_HEADS * A_DV) ** -0.5),
        "b_w_in": nrm((N_B_LAYERS, D, b_in_cols), D ** -0.5),
        "b_b_in": nrm((N_B_LAYERS, b_in_cols), 0.02),
        "b_sinks": nrm((N_B_LAYERS, B_HEADS), 1.0),
        "b_w_out": nrm((N_B_LAYERS, B_HEADS * B_HD, D), (B_HEADS * B_HD) ** -0.5),
        "b_b_out": nrm((N_B_LAYERS, D), 0.02),
        "final_norm": gain((D,)),
    }


def reference(x, ffn1_norm, ffn1_w_gu, ffn1_w_down, mix_norm, ffn2_norm, ffn2_w_gu, ffn2_w_down,
              a_w_in, a_w_conv, a_A_log, a_dt_bias, a_out_norm, a_w_out,
              b_w_in, b_b_in, b_sinks, b_w_out, b_b_out, final_norm):
    for layer in range(DEPTH):
        x = x + 0.5 * swiglu(rmsnorm(x, ffn1_norm[layer]), ffn1_w_gu[layer], ffn1_w_down[layer])
        h = rmsnorm(x, mix_norm[layer])
        j = layer // N_MIXERS
        if layer % N_MIXERS == 0:
            y = mixer_gated_deltanet(h, a_w_in[j], a_w_conv[j], a_A_log[j], a_dt_bias[j],
                                     a_out_norm[j], a_w_out[j])
        else:
            y = mixer_sliding_window_sinks(h, b_w_in[j], b_b_in[j], b_sinks[j], b_w_out[j], b_b_out[j])
        x = x + y
        x = x + 0.5 * swiglu(rmsnorm(x, ffn2_norm[layer]), ffn2_w_gu[layer], ffn2_w_down[layer])
    return rmsnorm(x, final_norm)
```

```python
import functools

import jax
import jax.numpy as jnp
from jax import lax
from jax.experimental import pallas as pl
from jax.experimental.pallas import tpu as pltpu

F32 = jnp.float32
BF16 = jnp.bfloat16
EPS = 1e-6

LANES = 128
VMEM_LIMIT_BYTES = 56 * 2**20

GDN_HEADS = 8
GDN_DK = 128
GDN_CONV = 4
GDN_CHUNK = 128
SWA_HEADS = 16
SWA_KV_HEADS = 4
SWA_HD = 64
SWA_BLOCK = 128


def _cparams(semantics):
    return pltpu.CompilerParams(dimension_semantics=semantics, vmem_limit_bytes=VMEM_LIMIT_BYTES)


def _resident(shape):
    return pl.BlockSpec(shape, lambda *_: (0,) * len(shape), pipeline_mode=pl.Buffered(1))


def _rms(x, w):
    ms = jnp.mean(x * x, axis=-1, keepdims=True)
    return x * lax.rsqrt(ms + EPS) * w


def _silu(x):
    return x / (1.0 + jnp.exp(-x))


def _mm(a, b):
    return jnp.dot(a.astype(BF16), b.astype(BF16), preferred_element_type=F32)


def _ffn_body(x_ref, nw_ref, wgu_ref, wd_ref, fw_ref, o_ref, *, d_ff, final_norm):
    x = x_ref[...]
    xn = _rms(x, nw_ref[...]).astype(BF16)
    gu = jnp.dot(xn, wgu_ref[...], preferred_element_type=F32)
    h = (_silu(gu[:, :d_ff]) * gu[:, d_ff:]).astype(BF16)
    y = jnp.dot(h, wd_ref[...], preferred_element_type=F32)
    out = x + 0.5 * y
    if final_norm:
        out = _rms(out, fw_ref[...])
    o_ref[...] = out


def _ffn(x, norm_w, w_gu, w_down, final_w, *, final_norm, tm):
    m, d = x.shape
    d_ff = w_down.shape[0]
    row = pl.BlockSpec((tm, d), lambda i: (i, 0))
    return pl.pallas_call(
        functools.partial(_ffn_body, d_ff=d_ff, final_norm=final_norm),
        grid=(m // tm,),
        in_specs=[row, _resident((1, d)), _resident((d, 2 * d_ff)), _resident((d_ff, d)),
                  _resident((1, d))],
        out_specs=row,
        out_shape=jax.ShapeDtypeStruct((m, d), F32),
        compiler_params=_cparams(("arbitrary",)),
        name="ffn",
    )(x, norm_w, w_gu, w_down, final_w)


def _norm_proj_body(x_ref, nw_ref, w_ref, b_ref, o_ref):
    xn = _rms(x_ref[...], nw_ref[...]).astype(BF16)
    y = jnp.dot(xn, w_ref[...], preferred_element_type=F32) + b_ref[...]
    o_ref[...] = y.astype(o_ref.dtype)


def _norm_proj(x, norm_w, w, b, *, tm):
    m, d = x.shape
    n = w.shape[1]
    return pl.pallas_call(
        _norm_proj_body,
        grid=(m // tm,),
        in_specs=[pl.BlockSpec((tm, d), lambda i: (i, 0)), _resident((1, d)), _resident((d, n)),
                  _resident((1, n))],
        out_specs=pl.BlockSpec((tm, n), lambda i: (i, 0)),
        out_shape=jax.ShapeDtypeStruct((m, n), BF16),
        compiler_params=_cparams(("arbitrary",)),
        name="norm_proj",
    )(x, norm_w, w, b)


def _proj_residual_body(o_ref, w_ref, b_ref, x_ref, y_ref):
    y = jnp.dot(o_ref[...], w_ref[...], preferred_element_type=F32)
    y_ref[...] = x_ref[...] + y + b_ref[...]


def _proj_residual(o, w, b, x, *, tm):
    m, k = o.shape
    d = w.shape[1]
    return pl.pallas_call(
        _proj_residual_body,
        grid=(m // tm,),
        in_specs=[pl.BlockSpec((tm, k), lambda i: (i, 0)), _resident((k, d)), _resident((1, d)),
                  pl.BlockSpec((tm, d), lambda i: (i, 0))],
        out_specs=pl.BlockSpec((tm, d), lambda i: (i, 0)),
        out_shape=jax.ShapeDtypeStruct((m, d), F32),
        compiler_params=_cparams(("arbitrary",)),
        name="proj_residual",
    )(o, w, b, x)


def _chunk_cumsum(g, chunk):
    pos = lax.broadcasted_iota(jnp.int32, g.shape, 0) % chunk
    shift = 1
    while shift < chunk:
        g = g + jnp.where(pos >= shift, pltpu.roll(g, shift, axis=0), 0.0)
        shift *= 2
    return g


def _softplus(x):
    return jnp.maximum(x, 0.0) + jnp.log(1.0 + jnp.exp(-jnp.abs(x)))


def _gdn_in_body(x_ref, nw_ref, wm_ref, wba_ref, wc_ref, alog_ref, dtb_ref,
                 qkvz_ref, aux_ref, pre_buf, *, tm, n_heads, dk, chunk):
    hk = n_heads * dk
    halo = 8
    xn = _rms(x_ref[...], nw_ref[...]).astype(BF16)
    pre = jnp.dot(xn, wm_ref[...], preferred_element_type=F32)

    @pl.when(pl.program_id(1) == 0)
    def _():
        pre_buf[0:halo, :] = jnp.zeros((halo, 3 * hk), F32)

    pre_buf[halo:halo + tm, :] = pre[:, :3 * hk]
    wc = wc_ref[...]
    acc = wc[GDN_CONV - 1:GDN_CONV, :] * pre[:, :3 * hk]
    for tap in range(GDN_CONV - 1):
        back = GDN_CONV - 1 - tap
        acc = acc + wc[tap:tap + 1, :] * pre_buf[halo - back:halo - back + tm, :]
    pre_buf[0:halo, :] = pre_buf[tm:tm + halo, :]
    c = _silu(acc)

    for h in range(n_heads):
        qs = c[:, h * dk:(h + 1) * dk]
        qn = qs * lax.rsqrt(jnp.sum(qs * qs, axis=-1, keepdims=True) + EPS) * (dk ** -0.5)
        qkvz_ref[:, h * dk:(h + 1) * dk] = qn.astype(BF16)
        ks = c[:, hk + h * dk:hk + (h + 1) * dk]
        kn = ks * lax.rsqrt(jnp.sum(ks * ks, axis=-1, keepdims=True) + EPS)
        qkvz_ref[:, hk + h * dk:hk + (h + 1) * dk] = kn.astype(BF16)
    qkvz_ref[:, 2 * hk:3 * hk] = c[:, 2 * hk:].astype(BF16)
    qkvz_ref[:, 3 * hk:] = pre[:, 3 * hk:].astype(BF16)

    ba = jnp.dot(xn, wba_ref[...], preferred_element_type=F32)
    g = -jnp.exp(alog_ref[...]) * _softplus(ba + dtb_ref[...])
    gc = _chunk_cumsum(g, chunk)
    beta = 1.0 / (1.0 + jnp.exp(-ba))
    lane = lax.broadcasted_iota(jnp.int32, ba.shape, 1) % LANES
    aux_ref[...] = jnp.where(lane < 2, gc, beta)


def _gdn_in(x, norm_w, w_main, w_ba, w_conv, alog, dtb, *, batch, tm):
    m, d = x.shape
    seq = m // batch
    nt = seq // tm
    n_main = w_main.shape[1]
    n_ba = w_ba.shape[1]
    hk = GDN_HEADS * GDN_DK
    row = lambda b, j: (b * nt + j, 0)
    return pl.pallas_call(
        functools.partial(_gdn_in_body, tm=tm, n_heads=GDN_HEADS, dk=GDN_DK, chunk=GDN_CHUNK),
        grid=(batch, nt),
        in_specs=[pl.BlockSpec((tm, d), row), _resident((1, d)), _resident((d, n_main)),
                  _resident((d, n_ba)), _resident((GDN_CONV, 3 * hk)), _resident((1, n_ba)),
                  _resident((1, n_ba))],
        out_specs=[pl.BlockSpec((tm, n_main), row), pl.BlockSpec((tm, n_ba), row)],
        out_shape=[jax.ShapeDtypeStruct((m, n_main), BF16), jax.ShapeDtypeStruct((m, n_ba), F32)],
        scratch_shapes=[pltpu.VMEM((tm + 8, 3 * hk), F32)],
        compiler_params=_cparams(("arbitrary", "arbitrary")),
        name="gdn_in",
    )(x, norm_w, w_main, w_ba, w_conv, alog, dtb)


def _bd(y, dk):
    zero = jnp.zeros((y.shape[0], dk), y.dtype)
    top = jnp.concatenate([y[:, :dk], zero], axis=1)
    bot = jnp.concatenate([zero, y[:, dk:]], axis=1)
    return jnp.concatenate([top, bot], axis=0)


def _pair_bcast(col_a, col_b, rows, dk):
    return jnp.concatenate([jnp.broadcast_to(col_a, (rows, dk)),
                            jnp.broadcast_to(col_b, (rows, dk))], axis=1)


def _gdn_chunk_body(q_ref, k_ref, v_ref, z_ref, aux_ref, onw_ref, o_ref, s_ref, *, n_chunks, dk):
    c_len = GDN_CHUNK
    w2 = 2 * dk

    @pl.when(pl.program_id(2) == 0)
    def _():
        s_ref[...] = jnp.zeros_like(s_ref)

    row = lax.broadcasted_iota(jnp.int32, (c_len, w2), 0)
    col = lax.broadcasted_iota(jnp.int32, (c_len, w2), 1) % dk
    lower = row >= col
    strict = row > col
    eye = jnp.where(row == col, 1.0, 0.0)
    srow = lax.broadcasted_iota(jnp.int32, (w2, w2), 0) // dk
    scol = lax.broadcasted_iota(jnp.int32, (w2, w2), 1) // dk
    same_head = srow == scol

    state = s_ref[...]
    for c in range(n_chunks):
        rows = slice(c * c_len, (c + 1) * c_len)
        q = q_ref[rows, :].astype(F32)
        k = k_ref[rows, :].astype(F32)
        v = v_ref[rows, :].astype(F32)
        aux = aux_ref[rows, :]
        gc = _pair_bcast(aux[:, 0:1], aux[:, 1:2], c_len, dk)
        beta = _pair_bcast(aux[:, 2:3], aux[:, 3:4], c_len, dk)
        gc_t = jnp.concatenate([gc[:, :dk].T, gc[:, dk:].T], axis=1)
        decay = jnp.where(lower, jnp.exp(jnp.minimum(gc - gc_t, 0.0)), 0.0)
        gc_last = gc[c_len - 1:c_len, :]
        e_gc = jnp.exp(gc)

        kb = k * beta
        k_bd = _bd(k.astype(BF16), dk)
        qk_kbk = lax.dot_general(jnp.concatenate([q, kb], axis=0).astype(BF16), k_bd,
                                 (((1,), (1,)), ((), ())), preferred_element_type=F32)
        a_qk = jnp.where(lower, qk_kbk[:c_len] * decay, 0.0)
        lmat = jnp.where(strict, qk_kbk[c_len:] * decay, 0.0)

        p1 = jnp.where(row // 8 == col // 8, lmat, 0.0)
        p2 = _mm(p1, _bd(p1.astype(BF16), dk))
        p4 = _mm(p2, _bd(p2.astype(BF16), dk))
        inv = eye - p1
        inv = inv + _mm(inv, _bd(p2.astype(BF16), dk))
        inv = inv + _mm(inv, _bd(p4.astype(BF16), dk))
        size = 8
        while size < c_len:
            off = jnp.where((row // (2 * size) == col // (2 * size)) & (row // size != col // size),
                            lmat, 0.0)
            t = _mm(off, _bd(inv.astype(BF16), dk))
            inv = inv - _mm(inv, _bd(t.astype(BF16), dk))
            size *= 2

        u = _mm(inv, _bd((v * beta).astype(BF16), dk))
        w = _mm(inv, _bd((kb * e_gc).astype(BF16), dk))
        q_dec = q * e_gc
        k_dec = k * jnp.exp(gc_last - gc)

        wq_s = _mm(jnp.concatenate([w, q_dec], axis=0), state)
        v_new = u - wq_s[:c_len]
        o = wq_s[c_len:] + _mm(a_qk, _bd(v_new.astype(BF16), dk))
        kv = lax.dot_general(k_dec.astype(BF16), v_new.astype(BF16), (((0,), (0,)), ((), ())),
                             preferred_element_type=F32)
        state = state * jnp.exp(gc_last) + jnp.where(same_head, kv, 0.0)

        ms = _pair_bcast(jnp.mean(o[:, :dk] * o[:, :dk], axis=-1, keepdims=True),
                         jnp.mean(o[:, dk:] * o[:, dk:], axis=-1, keepdims=True), c_len, dk)
        z = z_ref[rows, :].astype(F32)
        o_ref[rows, :] = (o * lax.rsqrt(ms + EPS) * onw_ref[...] * _silu(z)).astype(o_ref.dtype)

    s_ref[...] = state


def _gdn_chunks(qkvz, aux, out_norm2, *, batch, n_chunks):
    m = qkvz.shape[0]
    seq = m // batch
    dk = GDN_DK
    tb = n_chunks * GDN_CHUNK
    nt = seq // tb
    n_pairs = GDN_HEADS // 2

    def col_block(offset):
        return lambda b, p, j: (b * nt + j, offset + p)

    spec = lambda offset: pl.BlockSpec((tb, 2 * dk), col_block(offset))
    return pl.pallas_call(
        functools.partial(_gdn_chunk_body, n_chunks=n_chunks, dk=dk),
        grid=(batch, n_pairs, nt),
        in_specs=[spec(0), spec(n_pairs), spec(2 * n_pairs), spec(3 * n_pairs),
                  pl.BlockSpec((tb, LANES), col_block(0)), _resident((1, 2 * dk))],
        out_specs=spec(0),
        out_shape=jax.ShapeDtypeStruct((m, GDN_HEADS * dk), BF16),
        scratch_shapes=[pltpu.VMEM((2 * dk, 2 * dk), F32)],
        compiler_params=_cparams(("arbitrary", "arbitrary", "arbitrary")),
        name="gdn_chunks",
    )(qkvz, qkvz, qkvz, qkvz, aux, out_norm2)


def _swap_halves(x):
    half = x.shape[1] // 2
    return jnp.concatenate([x[:, half:], x[:, :half]], axis=1)


def _swa_body(sink_ref, q_ref, kp_ref, kc_ref, vp_ref, vc_ref, o_ref):
    blk = SWA_BLOCK
    group = SWA_HEADS // SWA_KV_HEADS
    first = pl.program_id(1) == 0
    qi = lax.broadcasted_iota(jnp.int32, (blk, 2 * blk), 0)
    kj = lax.broadcasted_iota(jnp.int32, (blk, 2 * blk), 1)
    rel = qi + blk - kj
    valid = (rel >= 0) & (rel < blk) & (jnp.logical_not(first) | (kj >= blk))
    lane = lax.broadcasted_iota(jnp.int32, (blk, LANES), 1)
    low_half = lane < SWA_HD

    for slab in range(SWA_KV_HEADS // 2):
        cols = slice(slab * LANES, (slab + 1) * LANES)
        kk = jnp.concatenate([kp_ref[:, cols], kc_ref[:, cols]], axis=0)
        vv = jnp.concatenate([vp_ref[:, cols], vc_ref[:, cols]], axis=0)
        kk_sw = _swap_halves(kk)
        vv_sw = _swap_halves(vv)
        for qs in range(group):
            qslab = slab * group + qs
            kv_half = qs // (group // 2)
            q2 = q_ref[:, qslab * LANES:(qslab + 1) * LANES]
            res = []
            for half in range(2):
                head = 2 * qslab + half
                qm = jnp.where(low_half if half == 0 else jnp.logical_not(low_half), q2,
                               jnp.zeros_like(q2))
                kmat = kk if half == kv_half else kk_sw
                vmat = vv if half == kv_half else vv_sw
                s = lax.dot_general(qm, kmat, (((1,), (1,)), ((), ())),
                                    preferred_element_type=F32) * (SWA_HD ** -0.5)
                s = jnp.where(valid, s, -jnp.inf)
                sink = sink_ref[head]
                mx = jnp.maximum(jnp.max(s, axis=-1, keepdims=True), sink)
                p = jnp.exp(s - mx)
                denom = jnp.sum(p, axis=-1, keepdims=True) + jnp.exp(sink - mx)
                pv = jnp.dot(p.astype(BF16), vmat, preferred_element_type=F32)
                res.append(pv / denom)
            o_ref[:, qslab * LANES:(qslab + 1) * LANES] = jnp.where(
                low_half, res[0], res[1]).astype(o_ref.dtype)


def _swa(qkv, sinks, *, batch):
    m = qkv.shape[0]
    blk = SWA_BLOCK
    nb = m // batch // blk
    hq = SWA_HEADS * SWA_HD
    hkv = SWA_KV_HEADS * SWA_HD
    cur = lambda col: (lambda b, n: (b * nb + n, col))
    prev = lambda col: (lambda b, n: (b * nb + jnp.maximum(n - 1, 0), col))
    return pl.pallas_call(
        _swa_body,
        grid=(batch, nb),
        in_specs=[pl.BlockSpec(memory_space=pltpu.SMEM),
                  pl.BlockSpec((blk, hq), cur(0)),
                  pl.BlockSpec((blk, hkv), prev(hq // hkv)),
                  pl.BlockSpec((blk, hkv), cur(hq // hkv)),
                  pl.BlockSpec((blk, hkv), prev(hq // hkv + 1)),
                  pl.BlockSpec((blk, hkv), cur(hq // hkv + 1))],
        out_specs=pl.BlockSpec((blk, hq), cur(0)),
        out_shape=jax.ShapeDtypeStruct((m, hq), BF16),
        compiler_params=_cparams(("arbitrary", "arbitrary")),
        name="swa",
    )(sinks, qkv, qkv, qkv, qkv, qkv)


def _pair_lanes(vec_a, vec_b, width):
    n_pairs = vec_a.shape[0] // 2
    out = jnp.zeros((n_pairs, width), F32)
    out = out.at[:, 0:2].set(vec_a.reshape(n_pairs, 2))
    out = out.at[:, 2:4].set(vec_b.reshape(n_pairs, 2))
    return out.reshape(1, n_pairs * width)


def _pair_cols(w_a, w_b, width):
    d, h = w_a.shape
    n_pairs = h // 2
    out = jnp.zeros((d, n_pairs, width), w_a.dtype)
    out = out.at[:, :, 0:2].set(w_a.reshape(d, n_pairs, 2))
    out = out.at[:, :, 2:4].set(w_b.reshape(d, n_pairs, 2))
    return out.reshape(d, n_pairs * width)


def kernel(x, ffn1_norm, ffn1_w_gu, ffn1_w_down, mix_norm, ffn2_norm, ffn2_w_gu, ffn2_w_down,
           a_w_in, a_w_conv, a_A_log, a_dt_bias, a_out_norm, a_w_out,
           b_w_in, b_b_in, b_sinks, b_w_out, b_b_out, final_norm):
    batch, seq, d = x.shape
    depth = ffn1_norm.shape[0]
    m = batch * seq
    tm = min(512, seq)
    xf = x.reshape(m, d)
    row = lambda v: v.reshape(1, -1).astype(F32)
    zero_bias = jnp.zeros((1, d), F32)

    for layer in range(depth):
        xf = _ffn(xf, row(ffn1_norm[layer]), ffn1_w_gu[layer].astype(BF16),
                  ffn1_w_down[layer].astype(BF16), row(final_norm), final_norm=False, tm=tm)
        j = layer // 2
        if layer % 2 == 0:
            hk = GDN_HEADS * GDN_DK
            w_in = a_w_in[j]
            w_main = w_in[:, :4 * hk].astype(BF16)
            w_b = w_in[:, 4 * hk:4 * hk + GDN_HEADS]
            w_a = w_in[:, 4 * hk + GDN_HEADS:]
            w_ba = _pair_cols(w_a, w_b, LANES).astype(BF16)
            zeros_h = jnp.zeros((GDN_HEADS,), F32)
            alog = _pair_lanes(a_A_log[j].astype(F32), zeros_h, LANES)
            dtb = _pair_lanes(a_dt_bias[j].astype(F32), zeros_h, LANES)
            qkvz, aux = _gdn_in(xf, row(mix_norm[layer]), w_main, w_ba, a_w_conv[j].astype(F32),
                                alog, dtb, batch=batch, tm=tm)
            onw = jnp.tile(row(a_out_norm[j]), (1, 2))
            n_chunks = min(4, seq // GDN_CHUNK)
            o = _gdn_chunks(qkvz, aux, onw, batch=batch, n_chunks=n_chunks)
            xf = _proj_residual(o, a_w_out[j].astype(BF16), zero_bias, xf, tm=tm)
        else:
            qkv = _norm_proj(xf, row(mix_norm[layer]), b_w_in[j].astype(BF16), row(b_b_in[j]), tm=tm)
            o = _swa(qkv, b_sinks[j].astype(F32), batch=batch)
            xf = _proj_residual(o, b_w_out[j].astype(BF16), row(b_b_out[j]), xf, tm=tm)
        last = layer == depth - 1
        xf = _ffn(xf, row(ffn2_norm[layer]), ffn2_w_gu[layer].astype(BF16),
                  ffn2_w_down[layer].astype(BF16), row(final_norm), final_norm=last, tm=tm)
    return xf.reshape(batch, seq, d)
```

```python
import functools

import jax
import jax.numpy as jnp
from jax import lax
from jax.experimental import pallas as pl
from jax.experimental.pallas import tpu as pltpu

F32 = jnp.float32
BF16 = jnp.bfloat16
EPS = 1e-6

LANES = 128
VMEM_LIMIT_BYTES = 56 * 2**20

GDN_HEADS = 8
GDN_DK = 128
GDN_CONV = 4
GDN_CHUNK = 128
SWA_HEADS = 16
SWA_KV_HEADS = 4
SWA_HD = 64
SWA_BLOCK = 128


def _cparams(semantics):
    return pltpu.CompilerParams(dimension_semantics=semantics, vmem_limit_bytes=VMEM_LIMIT_BYTES)


def _resident(shape):
    return pl.BlockSpec(shape, lambda *_: (0,) * len(shape), pipeline_mode=pl.Buffered(1))


def _rms(x, w):
    ms = jnp.mean(x * x, axis=-1, keepdims=True)
    return x * lax.rsqrt(ms + EPS) * w


def _silu(x):
    return x / (1.0 + jnp.exp(-x))


def _mm(a, b):
    return jnp.dot(a.astype(BF16), b.astype(BF16), preferred_element_type=F32)


def _ffn_body(x_ref, nw_ref, wgu_ref, wd_ref, fw_ref, o_ref, *, d_ff, final_norm):
    x = x_ref[...]
    xn = _rms(x, nw_ref[...]).astype(BF16)
    gu = jnp.dot(xn, wgu_ref[...], preferred_element_type=F32)
    h = (_silu(gu[:, :d_ff]) * gu[:, d_ff:]).astype(BF16)
    y = jnp.dot(h, wd_ref[...], preferred_element_type=F32)
    out = x + 0.5 * y
    if final_norm:
        out = _rms(out, fw_ref[...])
    o_ref[...] = out


def _ffn(x, norm_w, w_gu, w_down, final_w, *, final_norm, tm):
    m, d = x.shape
    d_ff = w_down.shape[0]
    row = pl.BlockSpec((tm, d), lambda i: (i, 0))
    return pl.pallas_call(
        functools.partial(_ffn_body, d_ff=d_ff, final_norm=final_norm),
        grid=(m // tm,),
        in_specs=[row, _resident((1, d)), _resident((d, 2 * d_ff)), _resident((d_ff, d)),
                  _resident((1, d))],
        out_specs=row,
        out_shape=jax.ShapeDtypeStruct((m, d), F32),
        compiler_params=_cparams(("arbitrary",)),
        name="ffn",
    )(x, norm_w, w_gu, w_down, final_w)


def _norm_proj_body(x_ref, nw_ref, w_ref, b_ref, o_ref):
    xn = _rms(x_ref[...], nw_ref[...]).astype(BF16)
    y = jnp.dot(xn, w_ref[...], preferred_element_type=F32) + b_ref[...]
    o_ref[...] = y.astype(o_ref.dtype)


def _norm_proj(x, norm_w, w, b, *, tm):
    m, d = x.shape
    n = w.shape[1]
    return pl.pallas_call(
        _norm_proj_body,
        grid=(m // tm,),
        in_specs=[pl.BlockSpec((tm, d), lambda i: (i, 0)), _resident((1, d)), _resident((d, n)),
                  _resident((1, n))],
        out_specs=pl.BlockSpec((tm, n), lambda i: (i, 0)),
        out_shape=jax.ShapeDtypeStruct((m, n), BF16),
        compiler_params=_cparams(("arbitrary",)),
        name="norm_proj",
    )(x, norm_w, w, b)


def _proj_residual_body(o_ref, w_ref, b_ref, x_ref, y_ref):
    y = jnp.dot(o_ref[...], w_ref[...], preferred_element_type=F32)
    y_ref[...] = x_ref[...] + y + b_ref[...]


def _proj_residual(o, w, b, x, *, tm):
    m, k = o.shape
    d = w.shape[1]
    return pl.pallas_call(
        _proj_residual_body,
        grid=(m // tm,),
        in_specs=[pl.BlockSpec((tm, k), lambda i: (i, 0)), _resident((k, d)), _resident((1, d)),
                  pl.BlockSpec((tm, d), lambda i: (i, 0))],
        out_specs=pl.BlockSpec((tm, d), lambda i: (i, 0)),
        out_shape=jax.ShapeDtypeStruct((m, d), F32),
        compiler_params=_cparams(("arbitrary",)),
        name="proj_residual",
    )(o, w, b, x)


def _chunk_cumsum(g, chunk):
    pos = lax.broadcasted_iota(jnp.int32, g.shape, 0) % chunk
    shift = 1
    while shift < chunk:
        g = g + jnp.where(pos >= shift, pltpu.roll(g, shift, axis=0), 0.0)
        shift *= 2
    return g


def _softplus(x):
    return jnp.maximum(x, 0.0) + jnp.log(1.0 + jnp.exp(-jnp.abs(x)))


def _gdn_in_body(x_ref, nw_ref, wm_ref, wba_ref, wc_ref, alog_ref, dtb_ref,
                 qkvz_ref, aux_ref, pre_buf, *, tm, n_heads, dk, chunk):
    hk = n_heads * dk
    halo = 8
    xn = _rms(x_ref[...], nw_ref[...]).astype(BF16)
    pre = jnp.dot(xn, wm_ref[...], preferred_element_type=F32)

    @pl.when(pl.program_id(1) == 0)
    def _():
        pre_buf[0:halo, :] = jnp.zeros((halo, 3 * hk), F32)

    pre_buf[halo:halo + tm, :] = pre[:, :3 * hk]
    wc = wc_ref[...]
    acc = wc[GDN_CONV - 1:GDN_CONV, :] * pre[:, :3 * hk]
    for tap in range(GDN_CONV - 1):
        back = GDN_CONV - 1 - tap
        acc = acc + wc[tap:tap + 1, :] * pre_buf[halo - back:halo - back + tm, :]
    pre_buf[0:halo, :] = pre_buf[tm:tm + halo, :]
    c = _silu(acc)

    for h in range(n_heads):
        qs = c[:, h * dk:(h + 1) * dk]
        qn = qs * lax.rsqrt(jnp.sum(qs * qs, axis=-1, keepdims=True) + EPS) * (dk ** -0.5)
        qkvz_ref[:, h * dk:(h + 1) * dk] = qn.astype(BF16)
        ks = c[:, hk + h * dk:hk + (h + 1) * dk]
        kn = ks * lax.rsqrt(jnp.sum(ks * ks, axis=-1, keepdims=True) + EPS)
        qkvz_ref[:, hk + h * dk:hk + (h + 1) * dk] = kn.astype(BF16)
    qkvz_ref[:, 2 * hk:3 * hk] = c[:, 2 * hk:].astype(BF16)
    qkvz_ref[:, 3 * hk:] = pre[:, 3 * hk:].astype(BF16)

    ba = jnp.dot(xn, wba_ref[...], preferred_element_type=F32)
    g = -jnp.exp(alog_ref[...]) * _softplus(ba + dtb_ref[...])
    gc = _chunk_cumsum(g, chunk)
    beta = 1.0 / (1.0 + jnp.exp(-ba))
    lane = lax.broadcasted_iota(jnp.int32, ba.shape, 1) % LANES
    aux_ref[...] = jnp.where(lane < 2, gc, beta)


def _gdn_in(x, norm_w, w_main, w_ba, w_conv, alog, dtb, *, batch, tm):
    m, d = x.shape
    seq = m // batch
    nt = seq // tm
    n_main = w_main.shape[1]
    n_ba = w_ba.shape[1]
    hk = GDN_HEADS * GDN_DK
    row = lambda b, j: (b * nt + j, 0)
    return pl.pallas_call(
        functools.partial(_gdn_in_body, tm=tm, n_heads=GDN_HEADS, dk=GDN_DK, chunk=GDN_CHUNK),
        grid=(batch, nt),
        in_specs=[pl.BlockSpec((tm, d), row), _resident((1, d)), _resident((d, n_main)),
                  _resident((d, n_ba)), _resident((GDN_CONV, 3 * hk)), _resident((1, n_ba)),
                  _resident((1, n_ba))],
        out_specs=[pl.BlockSpec((tm, n_main), row), pl.BlockSpec((tm, n_ba), row)],
        out_shape=[jax.ShapeDtypeStruct((m, n_main), BF16), jax.ShapeDtypeStruct((m, n_ba), F32)],
        scratch_shapes=[pltpu.VMEM((tm + 8, 3 * hk), F32)],
        compiler_params=_cparams(("arbitrary", "arbitrary")),
        name="gdn_in",
    )(x, norm_w, w_main, w_ba, w_conv, alog, dtb)


def _bd(y, dk):
    zero = jnp.zeros((y.shape[0], dk), y.dtype)
    top = jnp.concatenate([y[:, :dk], zero], axis=1)
    bot = jnp.concatenate([zero, y[:, dk:]], axis=1)
    return jnp.concatenate([top, bot], axis=0)


def _pair_bcast(col_a, col_b, rows, dk):
    return jnp.concatenate([jnp.broadcast_to(col_a, (rows, dk)),
                            jnp.broadcast_to(col_b, (rows, dk))], axis=1)


def _gdn_chunk_body(q_ref, k_ref, v_ref, z_ref, aux_ref, onw_ref, o_ref, s_ref, *, n_chunks, dk):
    c_len = GDN_CHUNK
    w2 = 2 * dk
    n_batch = q_ref.shape[0]
    units = [(b, c) for b in range(n_batch) for c in range(n_chunks)]

    @pl.when(pl.program_id(1) == 0)
    def _():
        s_ref[...] = jnp.zeros_like(s_ref)

    row = lax.broadcasted_iota(jnp.int32, (c_len, w2), 0)
    col = lax.broadcasted_iota(jnp.int32, (c_len, w2), 1) % dk
    lower = row >= col
    strict = row > col
    eye = jnp.where(row == col, 1.0, 0.0)
    srow = lax.broadcasted_iota(jnp.int32, (w2, w2), 0) // dk
    scol = lax.broadcasted_iota(jnp.int32, (w2, w2), 1) // dk
    same_head = srow == scol
    bd = lambda y: _bd(y.astype(BF16), dk)
    rows_of = lambda c: slice(c * c_len, (c + 1) * c_len)

    q, k, kb, vb, gc, gc_last, e_gc, decay = [], [], [], [], [], [], [], []
    for b, c in units:
        aux = aux_ref[b, rows_of(c), :]
        g = _pair_bcast(aux[:, 0:1], aux[:, 1:2], c_len, dk)
        beta = _pair_bcast(aux[:, 2:3], aux[:, 3:4], c_len, dk)
        g_t = jnp.concatenate([g[:, :dk].T, g[:, dk:].T], axis=1)
        decay.append(jnp.where(lower, jnp.exp(jnp.minimum(g - g_t, 0.0)), 0.0))
        gc.append(g)
        gc_last.append(g[c_len - 1:c_len, :])
        e_gc.append(jnp.exp(g))
        q.append(q_ref[b, rows_of(c), :].astype(F32))
        k.append(k_ref[b, rows_of(c), :].astype(F32))
        kb.append(k[-1] * beta)
        vb.append(v_ref[b, rows_of(c), :].astype(F32) * beta)

    a_qk, lmat = [], []
    for i in range(len(units)):
        qk_kbk = lax.dot_general(jnp.concatenate([q[i], kb[i]], axis=0).astype(BF16), bd(k[i]),
                                 (((1,), (1,)), ((), ())), preferred_element_type=F32)
        a_qk.append(jnp.where(lower, qk_kbk[:c_len] * decay[i], 0.0))
        lmat.append(jnp.where(strict, qk_kbk[c_len:] * decay[i], 0.0))

    p1 = [jnp.where(row // 8 == col // 8, l, 0.0) for l in lmat]
    p2 = [_mm(p, bd(p)) for p in p1]
    p4 = [_mm(p, bd(p)) for p in p2]
    inv = [eye - p for p in p1]
    inv = [x + _mm(x, bd(p)) for x, p in zip(inv, p2)]
    inv = [x + _mm(x, bd(p)) for x, p in zip(inv, p4)]
    size = 8
    while size < c_len:
        sel = (row // (2 * size) == col // (2 * size)) & (row // size != col // size)
        t = [_mm(jnp.where(sel, l, 0.0), bd(x)) for l, x in zip(lmat, inv)]
        inv = [x - _mm(x, bd(y)) for x, y in zip(inv, t)]
        size *= 2

    u = [_mm(x, bd(y)) for x, y in zip(inv, vb)]
    w = [_mm(x, bd(kb[i] * e_gc[i])) for i, x in enumerate(inv)]
    wq = [jnp.concatenate([w[i], q[i] * e_gc[i]], axis=0).astype(BF16) for i in range(len(units))]
    k_dec = [(k[i] * jnp.exp(gc_last[i] - gc[i])).astype(BF16) for i in range(len(units))]

    state = [s_ref[b] for b in range(n_batch)]
    for c in range(n_chunks):
        idx = [b * n_chunks + c for b in range(n_batch)]
        wq_s = [_mm(wq[i], state[b]) for b, i in enumerate(idx)]
        v_new = [u[i] - wq_s[b][:c_len] for b, i in enumerate(idx)]
        kv = [lax.dot_general(k_dec[i], v_new[b].astype(BF16), (((0,), (0,)), ((), ())),
                              preferred_element_type=F32) for b, i in enumerate(idx)]
        state = [state[b] * jnp.exp(gc_last[i]) + jnp.where(same_head, kv[b], 0.0)
                 for b, i in enumerate(idx)]
        for b, i in enumerate(idx):
            o = wq_s[b][c_len:] + _mm(a_qk[i], bd(v_new[b]))
            ms = _pair_bcast(jnp.mean(o[:, :dk] * o[:, :dk], axis=-1, keepdims=True),
                             jnp.mean(o[:, dk:] * o[:, dk:], axis=-1, keepdims=True), c_len, dk)
            z = z_ref[b, rows_of(c), :].astype(F32)
            o_ref[b, rows_of(c), :] = (o * lax.rsqrt(ms + EPS) * onw_ref[...] * _silu(z)).astype(o_ref.dtype)

    for b in range(n_batch):
        s_ref[b] = state[b]


def _gdn_chunks(qkvz, aux, out_norm2, *, n_chunks):
    batch, seq, _ = qkvz.shape
    dk = GDN_DK
    tb = n_chunks * GDN_CHUNK
    n_pairs = GDN_HEADS // 2
    spec = lambda offset: pl.BlockSpec((batch, tb, 2 * dk), lambda p, j: (0, j, offset + p))
    return pl.pallas_call(
        functools.partial(_gdn_chunk_body, n_chunks=n_chunks, dk=dk),
        grid=(n_pairs, seq // tb),
        in_specs=[spec(0), spec(n_pairs), spec(2 * n_pairs), spec(3 * n_pairs),
                  pl.BlockSpec((batch, tb, LANES), lambda p, j: (0, j, p)), _resident((1, 2 * dk))],
        out_specs=spec(0),
        out_shape=jax.ShapeDtypeStruct((batch, seq, GDN_HEADS * dk), BF16),
        scratch_shapes=[pltpu.VMEM((batch, 2 * dk, 2 * dk), F32)],
        compiler_params=_cparams(("arbitrary", "arbitrary")),
        name="gdn_chunks",
    )(qkvz, qkvz, qkvz, qkvz, aux, out_norm2)


def _swap_halves(x):
    half = x.shape[1] // 2
    return jnp.concatenate([x[:, half:], x[:, :half]], axis=1)


def _swa_body(sink_ref, q_ref, kp_ref, kc_ref, vp_ref, vc_ref, o_ref):
    blk = SWA_BLOCK
    group = SWA_HEADS // SWA_KV_HEADS
    first = pl.program_id(1) == 0
    qi = lax.broadcasted_iota(jnp.int32, (blk, 2 * blk), 0)
    kj = lax.broadcasted_iota(jnp.int32, (blk, 2 * blk), 1)
    rel = qi + blk - kj
    valid = (rel >= 0) & (rel < blk) & (jnp.logical_not(first) | (kj >= blk))
    lane = lax.broadcasted_iota(jnp.int32, (blk, LANES), 1)
    low_half = lane < SWA_HD

    for slab in range(SWA_KV_HEADS // 2):
        cols = slice(slab * LANES, (slab + 1) * LANES)
        kk = jnp.concatenate([kp_ref[:, cols], kc_ref[:, cols]], axis=0)
        vv = jnp.concatenate([vp_ref[:, cols], vc_ref[:, cols]], axis=0)
        kk_sw = _swap_halves(kk)
        vv_sw = _swap_halves(vv)
        for qs in range(group):
            qslab = slab * group + qs
            kv_half = qs // (group // 2)
            q2 = q_ref[:, qslab * LANES:(qslab + 1) * LANES]
            res = []
            for half in range(2):
                head = 2 * qslab + half
                qm = jnp.where(low_half if half == 0 else jnp.logical_not(low_half), q2,
                               jnp.zeros_like(q2))
                kmat = kk if half == kv_half else kk_sw
                vmat = vv if half == kv_half else vv_sw
                s = lax.dot_general(qm, kmat, (((1,), (1,)), ((), ())),
                                    preferred_element_type=F32) * (SWA_HD ** -0.5)
                s = jnp.where(valid, s, -jnp.inf)
                sink = sink_ref[head]
                mx = jnp.maximum(jnp.max(s, axis=-1, keepdims=True), sink)
                p = jnp.exp(s - mx)
                denom = jnp.sum(p, axis=-1, keepdims=True) + jnp.exp(sink - mx)
                pv = jnp.dot(p.astype(BF16), vmat, preferred_element_type=F32)
                res.append(pv / denom)
            o_ref[:, qslab * LANES:(qslab + 1) * LANES] = jnp.where(
                low_half, res[0], res[1]).astype(o_ref.dtype)


def _swa(qkv, sinks, *, batch):
    m = qkv.shape[0]
    blk = SWA_BLOCK
    nb = m // batch // blk
    hq = SWA_HEADS * SWA_HD
    hkv = SWA_KV_HEADS * SWA_HD
    cur = lambda col: (lambda b, n: (b * nb + n, col))
    prev = lambda col: (lambda b, n: (b * nb + jnp.maximum(n - 1, 0), col))
    return pl.pallas_call(
        _swa_body,
        grid=(batch, nb),
        in_specs=[pl.BlockSpec(memory_space=pltpu.SMEM),
                  pl.BlockSpec((blk, hq), cur(0)),
                  pl.BlockSpec((blk, hkv), prev(hq // hkv)),
                  pl.BlockSpec((blk, hkv), cur(hq // hkv)),
                  pl.BlockSpec((blk, hkv), prev(hq // hkv + 1)),
                  pl.BlockSpec((blk, hkv), cur(hq // hkv + 1))],
        out_specs=pl.BlockSpec((blk, hq), cur(0)),
        out_shape=jax.ShapeDtypeStruct((m, hq), BF16),
        compiler_params=_cparams(("arbitrary", "arbitrary")),
        name="swa",
    )(sinks, qkv, qkv, qkv, qkv, qkv)


def _pair_lanes(vec_a, vec_b, width):
    n_pairs = vec_a.shape[0] // 2
    out = jnp.zeros((n_pairs, width), F32)
    out = out.at[:, 0:2].set(vec_a.reshape(n_pairs, 2))
    out = out.at[:, 2:4].set(vec_b.reshape(n_pairs, 2))
    return out.reshape(1, n_pairs * width)


def _pair_cols(w_a, w_b, width):
    d, h = w_a.shape
    n_pairs = h // 2
    out = jnp.zeros((d, n_pairs, width), w_a.dtype)
    out = out.at[:, :, 0:2].set(w_a.reshape(d, n_pairs, 2))
    out = out.at[:, :, 2:4].set(w_b.reshape(d, n_pairs, 2))
    return out.reshape(d, n_pairs * width)


def kernel(x, ffn1_norm, ffn1_w_gu, ffn1_w_down, mix_norm, ffn2_norm, ffn2_w_gu, ffn2_w_down,
           a_w_in, a_w_conv, a_A_log, a_dt_bias, a_out_norm, a_w_out,
           b_w_in, b_b_in, b_sinks, b_w_out, b_b_out, final_norm):
    batch, seq, d = x.shape
    depth = ffn1_norm.shape[0]
    m = batch * seq
    tm = min(512, seq)
    xf = x.reshape(m, d)
    row = lambda v: v.reshape(1, -1).astype(F32)
    zero_bias = jnp.zeros((1, d), F32)

    for layer in range(depth):
        xf = _ffn(xf, row(ffn1_norm[layer]), ffn1_w_gu[layer].astype(BF16),
                  ffn1_w_down[layer].astype(BF16), row(final_norm), final_norm=False, tm=tm)
        j = layer // 2
        if layer % 2 == 0:
            hk = GDN_HEADS * GDN_DK
            w_in = a_w_in[j]
            w_main = w_in[:, :4 * hk].astype(BF16)
            w_b = w_in[:, 4 * hk:4 * hk + GDN_HEADS]
            w_a = w_in[:, 4 * hk + GDN_HEADS:]
            w_ba = _pair_cols(w_a, w_b, LANES).astype(BF16)
            zeros_h = jnp.zeros((GDN_HEADS,), F32)
            alog = _pair_lanes(a_A_log[j].astype(F32), zeros_h, LANES)
            dtb = _pair_lanes(a_dt_bias[j].astype(F32), zeros_h, LANES)
            qkvz, aux = _gdn_in(xf, row(mix_norm[layer]), w_main, w_ba, a_w_conv[j].astype(F32),
                                alog, dtb, batch=batch, tm=tm)
            onw = jnp.tile(row(a_out_norm[j]), (1, 2))
            n_chunks = min(4, seq // GDN_CHUNK)
            o = _gdn_chunks(qkvz.reshape(batch, seq, -1), aux.reshape(batch, seq, -1), onw,
                            n_chunks=n_chunks)
            xf = _proj_residual(o.reshape(m, -1), a_w_out[j].astype(BF16), zero_bias, xf, tm=tm)
        else:
            qkv = _norm_proj(xf, row(mix_norm[layer]), b_w_in[j].astype(BF16), row(b_b_in[j]), tm=tm)
            o = _swa(qkv, b_sinks[j].astype(F32), batch=batch)
            xf = _proj_residual(o, b_w_out[j].astype(BF16), row(b_b_out[j]), xf, tm=tm)
        last = layer == depth - 1
        xf = _ffn(xf, row(ffn2_norm[layer]), ffn2_w_gu[layer].astype(BF16),
                  ffn2_w_down[layer].astype(BF16), row(final_norm), final_norm=last, tm=tm)
    return xf.reshape(batch, seq, d)
```

```python
import functools

import jax
import jax.numpy as jnp
from jax import lax
from jax.experimental import pallas as pl
from jax.experimental.pallas import tpu as pltpu

F32 = jnp.float32
BF16 = jnp.bfloat16
EPS = 1e-6

LANES = 128
VMEM_LIMIT_BYTES = 56 * 2**20

GDN_HEADS = 8
GDN_DK = 128
GDN_CONV = 4
GDN_CHUNK = 128
SWA_HEADS = 16
SWA_KV_HEADS = 4
SWA_HD = 64
SWA_BLOCK = 128


def _cparams(semantics):
    return pltpu.CompilerParams(dimension_semantics=semantics, vmem_limit_bytes=VMEM_LIMIT_BYTES)


def _resident(shape):
    return pl.BlockSpec(shape, lambda *_: (0,) * len(shape), pipeline_mode=pl.Buffered(1))


def _rms(x, w):
    ms = jnp.mean(x * x, axis=-1, keepdims=True)
    return x * lax.rsqrt(ms + EPS) * w


def _silu(x):
    return x / (1.0 + jnp.exp(-x))


def _mm(a, b):
    return jnp.dot(a.astype(BF16), b.astype(BF16), preferred_element_type=F32)


def _ffn_body(*refs, d_ff, has_pre, tail):
    refs = list(refs)
    x_ref = refs.pop(0)
    x = x_ref[...]
    if has_pre:
        o_ref, wo_ref, bo_ref = refs.pop(0), refs.pop(0), refs.pop(0)
        x = x + jnp.dot(o_ref[...], wo_ref[...], preferred_element_type=F32) + bo_ref[...]
    nw_ref, wgu_ref, wd_ref = refs.pop(0), refs.pop(0), refs.pop(0)
    xn = _rms(x, nw_ref[...]).astype(BF16)
    gu = jnp.dot(xn, wgu_ref[...], preferred_element_type=F32)
    h = (_silu(gu[:, :d_ff]) * gu[:, d_ff:]).astype(BF16)
    out = x + 0.5 * jnp.dot(h, wd_ref[...], preferred_element_type=F32)
    if tail == "norm":
        fw_ref, y_ref = refs
        y_ref[...] = _rms(out, fw_ref[...])
    elif tail == "proj":
        nw2_ref, wp_ref, bp_ref, y_ref, p_ref = refs
        y_ref[...] = out
        hn = _rms(out, nw2_ref[...]).astype(BF16)
        p_ref[...] = (jnp.dot(hn, wp_ref[...], preferred_element_type=F32) + bp_ref[...]).astype(p_ref.dtype)
    else:
        (y_ref,) = refs
        y_ref[...] = out


def _ffn(x, norm_w, w_gu, w_down, *, tm, pre=None, final_w=None, proj=None):
    m, d = x.shape
    d_ff = w_down.shape[0]
    row = lambda n: pl.BlockSpec((tm, n), lambda i: (i, 0))
    args, specs = [x], [row(d)]
    if pre is not None:
        o, w_o, b_o = pre
        args += [o, w_o, b_o]
        specs += [row(o.shape[1]), _resident(w_o.shape), _resident((1, d))]
    args += [norm_w, w_gu, w_down]
    specs += [_resident((1, d)), _resident((d, 2 * d_ff)), _resident((d_ff, d))]
    out_specs, out_shape = row(d), jax.ShapeDtypeStruct((m, d), F32)
    tail = "plain"
    if final_w is not None:
        tail = "norm"
        args.append(final_w)
        specs.append(_resident((1, d)))
    elif proj is not None:
        tail = "proj"
        nw2, w_p, b_p = proj
        n_p = w_p.shape[1]
        args += [nw2, w_p, b_p]
        specs += [_resident((1, d)), _resident((d, n_p)), _resident((1, n_p))]
        out_specs = [row(d), row(n_p)]
        out_shape = [out_shape, jax.ShapeDtypeStruct((m, n_p), BF16)]
    return pl.pallas_call(
        functools.partial(_ffn_body, d_ff=d_ff, has_pre=pre is not None, tail=tail),
        grid=(m // tm,),
        in_specs=specs,
        out_specs=out_specs,
        out_shape=out_shape,
        compiler_params=_cparams(("arbitrary",)),
        name="ffn",
    )(*args)


def _chunk_cumsum(g, chunk):
    pos = lax.broadcasted_iota(jnp.int32, g.shape, 0) % chunk
    shift = 1
    while shift < chunk:
        g = g + jnp.where(pos >= shift, pltpu.roll(g, shift, axis=0), 0.0)
        shift *= 2
    return g


def _softplus(x):
    return jnp.maximum(x, 0.0) + jnp.log(1.0 + jnp.exp(-jnp.abs(x)))


def _gdn_in_body(x_ref, nw_ref, wm_ref, wba_ref, wc_ref, alog_ref, dtb_ref,
                 qkvz_ref, aux_ref, carry, *, tm, n_heads, dk, chunk, cb):
    hk = n_heads * dk
    halo = carry.shape[0]
    xn = _rms(x_ref[...], nw_ref[...]).astype(BF16)

    @pl.when(pl.program_id(1) == 0)
    def _():
        carry[...] = jnp.zeros_like(carry)

    for j in range(3 * hk // cb):
        cols = slice(j * cb, (j + 1) * cb)
        pre = jnp.dot(xn, wm_ref[:, cols], preferred_element_type=F32)
        ext = jnp.concatenate([carry[:, cols], pre], axis=0)
        carry[:, cols] = pre[tm - halo:, :]
        wc = wc_ref[:, cols]
        ext1 = pltpu.roll(ext, 1, axis=0)
        lo = wc[1:2, :] * ext + wc[0:1, :] * ext1
        hi = wc[3:4, :] * ext + wc[2:3, :] * ext1
        acc = (hi + pltpu.roll(lo, 2, axis=0))[halo:, :]
        c = _silu(acc)
        if j * cb < 2 * hk:
            gain = dk ** -0.5 if j * cb < hk else 1.0
            for h in range(cb // dk):
                cs = c[:, h * dk:(h + 1) * dk]
                cn = cs * (lax.rsqrt(jnp.sum(cs * cs, axis=-1, keepdims=True) + EPS) * gain)
                qkvz_ref[:, j * cb + h * dk:j * cb + (h + 1) * dk] = cn.astype(BF16)
        else:
            qkvz_ref[:, cols] = c.astype(BF16)
    for j in range(3 * hk // cb, 4 * hk // cb):
        cols = slice(j * cb, (j + 1) * cb)
        qkvz_ref[:, cols] = jnp.dot(xn, wm_ref[:, cols], preferred_element_type=F32).astype(BF16)

    ba = jnp.dot(xn, wba_ref[...], preferred_element_type=F32)
    g = -jnp.exp(alog_ref[...]) * _softplus(ba + dtb_ref[...])
    gc = _chunk_cumsum(g, chunk)
    beta = 1.0 / (1.0 + jnp.exp(-ba))
    lane = lax.broadcasted_iota(jnp.int32, ba.shape, 1)
    packed = jnp.where(lane % 4 < 2, gc, beta)
    for p in range(n_heads // 2):
        aux_ref[:, p * LANES:(p + 1) * LANES] = packed if p == 0 else pltpu.roll(packed, LANES - 4 * p, axis=1)


def _gdn_in(x, norm_w, w_main, w_ba, w_conv, alog, dtb, *, batch, tm):
    m, d = x.shape
    seq = m // batch
    nt = seq // tm
    n_main = w_main.shape[1]
    hk = GDN_HEADS * GDN_DK
    n_aux = GDN_HEADS // 2 * LANES
    row = lambda b, j: (b * nt + j, 0)
    return pl.pallas_call(
        functools.partial(_gdn_in_body, tm=tm, n_heads=GDN_HEADS, dk=GDN_DK, chunk=GDN_CHUNK, cb=512),
        grid=(batch, nt),
        in_specs=[pl.BlockSpec((tm, d), row), _resident((1, d)), _resident((d, n_main)),
                  _resident((d, LANES)), _resident((GDN_CONV, 3 * hk)), _resident((1, LANES)),
                  _resident((1, LANES))],
        out_specs=[pl.BlockSpec((tm, n_main), row), pl.BlockSpec((tm, n_aux), row)],
        out_shape=[jax.ShapeDtypeStruct((m, n_main), BF16), jax.ShapeDtypeStruct((m, n_aux), F32)],
        scratch_shapes=[pltpu.VMEM((8, 3 * hk), F32)],
        compiler_params=_cparams(("arbitrary", "arbitrary")),
        name="gdn_in",
    )(x, norm_w, w_main, w_ba, w_conv, alog, dtb)


def _bd(y, dk):
    zero = jnp.zeros((y.shape[0], dk), y.dtype)
    top = jnp.concatenate([y[:, :dk], zero], axis=1)
    bot = jnp.concatenate([zero, y[:, dk:]], axis=1)
    return jnp.concatenate([top, bot], axis=0)


def _pair_bcast(col_a, col_b, rows, dk):
    return jnp.concatenate([jnp.broadcast_to(col_a, (rows, dk)),
                            jnp.broadcast_to(col_b, (rows, dk))], axis=1)


def _gdn_chunk_body(q_ref, k_ref, v_ref, z_ref, aux_ref, onw_ref, o_ref, s_ref, *, n_chunks, dk):
    c_len = GDN_CHUNK
    w2 = 2 * dk
    n_batch = q_ref.shape[0]
    units = [(b, c) for b in range(n_batch) for c in range(n_chunks)]

    @pl.when(pl.program_id(1) == 0)
    def _():
        s_ref[...] = jnp.zeros_like(s_ref)

    row = lax.broadcasted_iota(jnp.int32, (c_len, w2), 0)
    col = lax.broadcasted_iota(jnp.int32, (c_len, w2), 1) % dk
    lower = row >= col
    strict = row > col
    eye = jnp.where(row == col, 1.0, 0.0)
    srow = lax.broadcasted_iota(jnp.int32, (w2, w2), 0) // dk
    scol = lax.broadcasted_iota(jnp.int32, (w2, w2), 1) // dk
    same_head = srow == scol
    bd = lambda y: _bd(y.astype(BF16), dk)
    rows_of = lambda c: slice(c * c_len, (c + 1) * c_len)

    q, k, kb, vb, gc, gc_last, e_gc, decay = [], [], [], [], [], [], [], []
    for b, c in units:
        aux = aux_ref[b, rows_of(c), :]
        g = _pair_bcast(aux[:, 0:1], aux[:, 1:2], c_len, dk)
        beta = _pair_bcast(aux[:, 2:3], aux[:, 3:4], c_len, dk)
        g_t = jnp.concatenate([g[:, :dk].T, g[:, dk:].T], axis=1)
        decay.append(jnp.where(lower, jnp.exp(jnp.minimum(g - g_t, 0.0)), 0.0))
        gc.append(g)
        gc_last.append(g[c_len - 1:c_len, :])
        e_gc.append(jnp.exp(g))
        q.append(q_ref[b, rows_of(c), :].astype(F32))
        k.append(k_ref[b, rows_of(c), :].astype(F32))
        kb.append(k[-1] * beta)
        vb.append(v_ref[b, rows_of(c), :].astype(F32) * beta)

    a_qk, lmat = [], []
    for i in range(len(units)):
        qk_kbk = lax.dot_general(jnp.concatenate([q[i], kb[i]], axis=0).astype(BF16), bd(k[i]),
                                 (((1,), (1,)), ((), ())), preferred_element_type=F32)
        a_qk.append(jnp.where(lower, qk_kbk[:c_len] * decay[i], 0.0))
        lmat.append(jnp.where(strict, qk_kbk[c_len:] * decay[i], 0.0))

    p1 = [jnp.where(row // 8 == col // 8, l, 0.0) for l in lmat]
    p2 = [_mm(p, bd(p)) for p in p1]
    p4 = [_mm(p, bd(p)) for p in p2]
    inv = [eye - p for p in p1]
    inv = [x + _mm(x, bd(p)) for x, p in zip(inv, p2)]
    inv = [x + _mm(x, bd(p)) for x, p in zip(inv, p4)]
    size = 8
    while size < c_len:
        sel = (row // (2 * size) == col // (2 * size)) & (row // size != col // size)
        t = [_mm(jnp.where(sel, l, 0.0), bd(x)) for l, x in zip(lmat, inv)]
        inv = [x - _mm(x, bd(y)) for x, y in zip(inv, t)]
        size *= 2

    u = [_mm(x, bd(y)) for x, y in zip(inv, vb)]
    w = [_mm(x, bd(kb[i] * e_gc[i])) for i, x in enumerate(inv)]
    wq = [jnp.concatenate([w[i], q[i] * e_gc[i]], axis=0).astype(BF16) for i in range(len(units))]
    k_dec = [(k[i] * jnp.exp(gc_last[i] - gc[i])).astype(BF16) for i in range(len(units))]

    state = [s_ref[b] for b in range(n_batch)]
    for c in range(n_chunks):
        idx = [b * n_chunks + c for b in range(n_batch)]
        wq_s = [_mm(wq[i], state[b]) for b, i in enumerate(idx)]
        v_new = [u[i] - wq_s[b][:c_len] for b, i in enumerate(idx)]
        kv = [lax.dot_general(k_dec[i], v_new[b].astype(BF16), (((0,), (0,)), ((), ())),
                              preferred_element_type=F32) for b, i in enumerate(idx)]
        state = [state[b] * jnp.exp(gc_last[i]) + jnp.where(same_head, kv[b], 0.0)
                 for b, i in enumerate(idx)]
        for b, i in enumerate(idx):
            o = wq_s[b][c_len:] + _mm(a_qk[i], bd(v_new[b]))
            ms = _pair_bcast(jnp.mean(o[:, :dk] * o[:, :dk], axis=-1, keepdims=True),
                             jnp.mean(o[:, dk:] * o[:, dk:], axis=-1, keepdims=True), c_len, dk)
            z = z_ref[b, rows_of(c), :].astype(F32)
            o_ref[b, rows_of(c), :] = (o * lax.rsqrt(ms + EPS) * onw_ref[...] * _silu(z)).astype(o_ref.dtype)

    for b in range(n_batch):
        s_ref[b] = state[b]


def _gdn_chunks(qkvz, aux, out_norm2, *, n_chunks):
    batch, seq, _ = qkvz.shape
    dk = GDN_DK
    tb = n_chunks * GDN_CHUNK
    n_pairs = GDN_HEADS // 2
    spec = lambda offset: pl.BlockSpec((batch, tb, 2 * dk), lambda p, j: (0, j, offset + p))
    return pl.pallas_call(
        functools.partial(_gdn_chunk_body, n_chunks=n_chunks, dk=dk),
        grid=(n_pairs, seq // tb),
        in_specs=[spec(0), spec(n_pairs), spec(2 * n_pairs), spec(3 * n_pairs),
                  pl.BlockSpec((batch, tb, LANES), lambda p, j: (0, j, p)), _resident((1, 2 * dk))],
        out_specs=spec(0),
        out_shape=jax.ShapeDtypeStruct((batch, seq, GDN_HEADS * dk), BF16),
        scratch_shapes=[pltpu.VMEM((batch, 2 * dk, 2 * dk), F32)],
        compiler_params=_cparams(("arbitrary", "arbitrary")),
        name="gdn_chunks",
    )(qkvz, qkvz, qkvz, qkvz, aux, out_norm2)


def _swap_halves(x):
    half = x.shape[1] // 2
    return jnp.concatenate([x[:, half:], x[:, :half]], axis=1)


def _swa_body(sink_ref, q_ref, kp_ref, kc_ref, vp_ref, vc_ref, o_ref, *, n_sub):
    blk = SWA_BLOCK
    first = pl.program_id(1) == 0
    qi = lax.broadcasted_iota(jnp.int32, (2 * blk, blk), 0) % blk
    kj = lax.broadcasted_iota(jnp.int32, (2 * blk, blk), 1)
    from_prev = kj > qi
    top = lax.broadcasted_iota(jnp.int32, (2 * blk, 1), 0) < blk
    lane = lax.broadcasted_iota(jnp.int32, (blk, LANES), 1)
    low_half = lane < SWA_HD
    scale = SWA_HD ** -0.5

    for slab in range(SWA_KV_HEADS // 2):
        cols = slice(slab * LANES, (slab + 1) * LANES)
        k_all = jnp.concatenate([kp_ref[:, cols], kc_ref[:, cols]], axis=0)
        v_all = jnp.concatenate([vp_ref[:, cols], vc_ref[:, cols]], axis=0)
        k_sw = _swap_halves(k_all)
        v_sw = _swap_halves(v_all)
        for sub in range(n_sub):
            rows = slice(sub * blk, (sub + 1) * blk)
            keys = slice(sub * blk, (sub + 2) * blk)
            no_prev = first if sub == 0 else None
            jobs = []
            for kv_half in range(2):
                qslab0 = 2 * (2 * slab + kv_half)
                for half in range(2):
                    keep = low_half if half == 0 else jnp.logical_not(low_half)
                    qm = [jnp.where(keep, q_ref[rows, (qslab0 + t) * LANES:(qslab0 + t + 1) * LANES] * scale, 0.0)
                          for t in range(2)]
                    kmat = (k_all if half == kv_half else k_sw)[keys]
                    s = lax.dot_general(jnp.concatenate(qm, axis=0).astype(BF16), kmat,
                                        (((1,), (1,)), ((), ())), preferred_element_type=F32)
                    jobs.append((kv_half, half, qslab0, s))
            outs = {}
            for kv_half, half, qslab0, s in jobs:
                s_prev = s[:, :blk]
                if no_prev is not None:
                    s_prev = jnp.where(no_prev, -jnp.inf, s_prev)
                sf = jnp.where(from_prev, s_prev, s[:, blk:])
                sink = jnp.where(top, sink_ref[2 * qslab0 + half], sink_ref[2 * qslab0 + 2 + half])
                mx = jnp.maximum(jnp.max(sf, axis=-1, keepdims=True), sink)
                p = jnp.exp(sf - mx)
                denom = jnp.sum(p, axis=-1, keepdims=True) + jnp.exp(sink - mx)
                p2 = jnp.concatenate([jnp.where(from_prev, p, 0.0), jnp.where(from_prev, 0.0, p)], axis=1)
                vmat = (v_all if half == kv_half else v_sw)[keys]
                outs[(qslab0, half)] = jnp.dot(p2.astype(BF16), vmat, preferred_element_type=F32) / denom
            for kv_half in range(2):
                qslab0 = 2 * (2 * slab + kv_half)
                for t in range(2):
                    part = slice(t * blk, (t + 1) * blk)
                    o_ref[rows, (qslab0 + t) * LANES:(qslab0 + t + 1) * LANES] = jnp.where(
                        low_half, outs[(qslab0, 0)][part], outs[(qslab0, 1)][part]).astype(o_ref.dtype)


def _swa(qkv, sinks, *, batch, n_sub):
    m = qkv.shape[0]
    blk = SWA_BLOCK
    rows = n_sub * blk
    nt = m // batch // rows
    hq = SWA_HEADS * SWA_HD
    hkv = SWA_KV_HEADS * SWA_HD
    cur = lambda col: (lambda b, n: (b * nt + n, col))
    prev = lambda col: (lambda b, n: ((b * nt + n) * n_sub - jnp.minimum(n, 1), col))
    return pl.pallas_call(
        functools.partial(_swa_body, n_sub=n_sub),
        grid=(batch, nt),
        in_specs=[pl.BlockSpec(memory_space=pltpu.SMEM),
                  pl.BlockSpec((rows, hq), cur(0)),
                  pl.BlockSpec((blk, hkv), prev(hq // hkv)),
                  pl.BlockSpec((rows, hkv), cur(hq // hkv)),
                  pl.BlockSpec((blk, hkv), prev(hq // hkv + 1)),
                  pl.BlockSpec((rows, hkv), cur(hq // hkv + 1))],
        out_specs=pl.BlockSpec((rows, hq), cur(0)),
        out_shape=jax.ShapeDtypeStruct((m, hq), BF16),
        compiler_params=_cparams(("arbitrary", "arbitrary")),
        name="swa",
    )(sinks, qkv, qkv, qkv, qkv, qkv)


def _pair_lanes(vec_a, vec_b):
    n_pairs = vec_a.shape[0] // 2
    quad = jnp.concatenate([vec_a.reshape(n_pairs, 2), vec_b.reshape(n_pairs, 2)], axis=1)
    return jnp.pad(quad.reshape(1, 4 * n_pairs), ((0, 0), (0, LANES - 4 * n_pairs)))


def _pair_cols(w_a, w_b):
    d, h = w_a.shape
    quad = jnp.concatenate([w_a.reshape(d, h // 2, 2), w_b.reshape(d, h // 2, 2)], axis=2)
    return jnp.pad(quad.reshape(d, 2 * h), ((0, 0), (0, LANES - 2 * h)))


def kernel(x, ffn1_norm, ffn1_w_gu, ffn1_w_down, mix_norm, ffn2_norm, ffn2_w_gu, ffn2_w_down,
           a_w_in, a_w_conv, a_A_log, a_dt_bias, a_out_norm, a_w_out,
           b_w_in, b_b_in, b_sinks, b_w_out, b_b_out, final_norm):
    batch, seq, d = x.shape
    depth = ffn1_norm.shape[0]
    m = batch * seq
    tm = min(512, seq)
    xf = x.reshape(m, d)
    row = lambda v: v.reshape(1, -1).astype(F32)
    cast = lambda w: w.astype(BF16)

    for layer in range(depth):
        j = layer // 2
        last = layer == depth - 1
        ffn1 = (row(ffn1_norm[layer]), cast(ffn1_w_gu[layer]), cast(ffn1_w_down[layer]))
        ffn2 = (row(ffn2_norm[layer]), cast(ffn2_w_gu[layer]), cast(ffn2_w_down[layer]))
        if layer % 2 == 0:
            xf = _ffn(xf, *ffn1, tm=tm)
            hk = GDN_HEADS * GDN_DK
            w_in = a_w_in[j]
            w_b = w_in[:, 4 * hk:4 * hk + GDN_HEADS]
            w_a = w_in[:, 4 * hk + GDN_HEADS:]
            zeros_h = jnp.zeros((GDN_HEADS,), F32)
            qkvz, aux = _gdn_in(xf, row(mix_norm[layer]), cast(w_in[:, :4 * hk]), cast(_pair_cols(w_a, w_b)),
                                a_w_conv[j].astype(F32), _pair_lanes(a_A_log[j].astype(F32), zeros_h),
                                _pair_lanes(a_dt_bias[j].astype(F32), zeros_h), batch=batch, tm=tm)
            o = _gdn_chunks(qkvz.reshape(batch, seq, -1), aux.reshape(batch, seq, -1),
                            jnp.tile(row(a_out_norm[j]), (1, 2)), n_chunks=min(8, seq // GDN_CHUNK))
            pre = (o.reshape(m, -1), cast(a_w_out[j]), jnp.zeros((1, d), F32))
        else:
            xf, qkv = _ffn(xf, *ffn1, tm=tm,
                           proj=(row(mix_norm[layer]), cast(b_w_in[j]), row(b_b_in[j])))
            o = _swa(qkv, b_sinks[j].astype(F32), batch=batch, n_sub=min(4, seq // SWA_BLOCK))
            pre = (o, cast(b_w_out[j]), row(b_b_out[j]))
        xf = _ffn(xf, *ffn2, tm=tm, pre=pre, final_w=row(final_norm) if last else None)
    return xf.reshape(batch, seq, d)
```

```python
import functools

import jax
import jax.numpy as jnp
from jax import lax
from jax.experimental import pallas as pl
from jax.experimental.pallas import tpu as pltpu

F32 = jnp.float32
BF16 = jnp.bfloat16
EPS = 1e-6

LANES = 128
VMEM_LIMIT_BYTES = 56 * 2**20

GDN_HEADS = 8
GDN_DK = 128
GDN_CONV = 4
GDN_CHUNK = 128
SWA_HEADS = 16
SWA_KV_HEADS = 4
SWA_HD = 64
SWA_BLOCK = 128


def _cparams(semantics):
    return pltpu.CompilerParams(dimension_semantics=semantics, vmem_limit_bytes=VMEM_LIMIT_BYTES)


def _resident(shape):
    return pl.BlockSpec(shape, lambda *_: (0,) * len(shape), pipeline_mode=pl.Buffered(1))


def _rms(x, w):
    ms = jnp.mean(x * x, axis=-1, keepdims=True)
    return x * lax.rsqrt(ms + EPS) * w


def _silu(x):
    return x / (1.0 + jnp.exp(-x))


def _mm(a, b):
    return jnp.dot(a.astype(BF16), b.astype(BF16), preferred_element_type=F32)


N_WCHUNK = 8


def _w_spec(w, layer, n_cols=None):
    _, n_rows, n_all = w.shape
    rows = n_rows // N_WCHUNK
    return pl.BlockSpec((None, rows, n_cols or n_all),
                        lambda i: (layer, jnp.minimum(i, N_WCHUNK - 1), 0))


def _w_scratch(w, n_cols=None):
    return pltpu.VMEM((w.shape[1], n_cols or w.shape[2]), BF16)


def _cast_in(i, pairs):
    @pl.when(i < N_WCHUNK)
    def _():
        for src, dst in pairs:
            rows = src.shape[0]
            dst[pl.ds(pl.multiple_of(i * rows, rows), rows), :] = src[...].astype(BF16)


def _tile_index(i):
    return jnp.maximum(i - N_WCHUNK, 0)


def _ffn_body(*refs, d_ff, has_pre, tail):
    i = pl.program_id(0)
    refs = list(refs)
    n_w = 2 + has_pre + (tail == "proj")
    w_s = refs[-n_w:]
    refs = refs[:-n_w]
    x_ref = refs.pop(0)
    casts = []
    if has_pre:
        o_ref, wo_ref, bo_ref = refs.pop(0), refs.pop(0), refs.pop(0)
        wo_s = w_s.pop(0)
        casts.append((wo_ref, wo_s))
    nw_ref, wgu_ref, wd_ref = refs.pop(0), refs.pop(0), refs.pop(0)
    wgu_s, wd_s = w_s.pop(0), w_s.pop(0)
    casts += [(wgu_ref, wgu_s), (wd_ref, wd_s)]
    if tail == "proj":
        nw2_ref, wp_ref, bp_ref, y_ref, p_ref = refs
        wp_s = w_s.pop(0)
        casts.append((wp_ref, wp_s))
    elif tail == "norm":
        fw_ref, y_ref = refs
    else:
        (y_ref,) = refs
    _cast_in(i, casts)

    @pl.when(i >= N_WCHUNK)
    def _():
        x = x_ref[...]
        if has_pre:
            x = x + jnp.dot(o_ref[...], wo_s[...], preferred_element_type=F32) + bo_ref[...]
        xn = _rms(x, nw_ref[...]).astype(BF16)
        gu = jnp.dot(xn, wgu_s[...], preferred_element_type=F32)
        h = (_silu(gu[:, :d_ff]) * gu[:, d_ff:]).astype(BF16)
        out = x + 0.5 * jnp.dot(h, wd_s[...], preferred_element_type=F32)
        if tail == "norm":
            y_ref[...] = _rms(out, fw_ref[...])
        else:
            y_ref[...] = out
        if tail == "proj":
            hn = _rms(out, nw2_ref[...]).astype(BF16)
            p_ref[...] = (jnp.dot(hn, wp_s[...], preferred_element_type=F32) + bp_ref[...]).astype(p_ref.dtype)


def _ffn(x, norm_w, w_gu, w_down, layer, *, tm, pre=None, final_w=None, proj=None):
    m, d = x.shape
    d_ff = w_down.shape[1]
    row = lambda n: pl.BlockSpec((tm, n), lambda i: (_tile_index(i), 0))
    args, specs, scratch = [x], [row(d)], []
    if pre is not None:
        o, w_o, j_o, b_o = pre
        args += [o, w_o, b_o]
        specs += [row(o.shape[1]), _w_spec(w_o, j_o), _resident((1, d))]
        scratch.append(_w_scratch(w_o))
    args += [norm_w, w_gu, w_down]
    specs += [_resident((1, d)), _w_spec(w_gu, layer), _w_spec(w_down, layer)]
    scratch += [_w_scratch(w_gu), _w_scratch(w_down)]
    out_specs, out_shape = row(d), jax.ShapeDtypeStruct((m, d), F32)
    tail = "plain"
    if final_w is not None:
        tail = "norm"
        args.append(final_w)
        specs.append(_resident((1, d)))
    elif proj is not None:
        tail = "proj"
        nw2, w_p, j_p, b_p = proj
        n_p = w_p.shape[2]
        args += [nw2, w_p, b_p]
        specs += [_resident((1, d)), _w_spec(w_p, j_p), _resident((1, n_p))]
        scratch.append(_w_scratch(w_p))
        out_specs = [row(d), row(n_p)]
        out_shape = [out_shape, jax.ShapeDtypeStruct((m, n_p), BF16)]
    return pl.pallas_call(
        functools.partial(_ffn_body, d_ff=d_ff, has_pre=pre is not None, tail=tail),
        grid=(N_WCHUNK + m // tm,),
        in_specs=specs,
        out_specs=out_specs,
        out_shape=out_shape,
        scratch_shapes=scratch,
        compiler_params=_cparams(("arbitrary",)),
        name="ffn",
    )(*args)


def _chunk_cumsum(g, chunk):
    pos = lax.broadcasted_iota(jnp.int32, g.shape, 0) % chunk
    shift = 1
    while shift < chunk:
        g = g + jnp.where(pos >= shift, pltpu.roll(g, shift, axis=0), 0.0)
        shift *= 2
    return g


def _softplus(x):
    return jnp.maximum(x, 0.0) + jnp.log(1.0 + jnp.exp(-jnp.abs(x)))


def _gdn_in_body(x_ref, nw_ref, wm_ref, wba_ref, wc_ref, alog_ref, dtb_ref,
                 qkvz_ref, aux_ref, carry, wm_s, *, tm, n_heads, dk, chunk, cb, tiles_per_seq):
    i = pl.program_id(0)
    _cast_in(i, [(wm_ref, wm_s)])

    @pl.when(i >= N_WCHUNK)
    def _():
        _gdn_in_tile(x_ref, nw_ref, wm_s, wba_ref, wc_ref, alog_ref, dtb_ref, qkvz_ref, aux_ref, carry,
                     first=(i - N_WCHUNK) % tiles_per_seq == 0, tm=tm, n_heads=n_heads, dk=dk,
                     chunk=chunk, cb=cb)


def _gdn_in_tile(x_ref, nw_ref, wm_ref, wba_ref, wc_ref, alog_ref, dtb_ref,
                 qkvz_ref, aux_ref, carry, *, first, tm, n_heads, dk, chunk, cb):
    hk = n_heads * dk
    halo = carry.shape[0]
    xn = _rms(x_ref[...], nw_ref[...]).astype(BF16)

    @pl.when(first)
    def _():
        carry[...] = jnp.zeros_like(carry)

    for j in range(3 * hk // cb):
        cols = slice(j * cb, (j + 1) * cb)
        pre = jnp.dot(xn, wm_ref[:, cols], preferred_element_type=F32)
        ext = jnp.concatenate([carry[:, cols], pre], axis=0)
        carry[:, cols] = pre[tm - halo:, :]
        wc = wc_ref[:, cols]
        ext1 = pltpu.roll(ext, 1, axis=0)
        lo = wc[1:2, :] * ext + wc[0:1, :] * ext1
        hi = wc[3:4, :] * ext + wc[2:3, :] * ext1
        acc = (hi + pltpu.roll(lo, 2, axis=0))[halo:, :]
        c = _silu(acc)
        if j * cb < 2 * hk:
            gain = dk ** -0.5 if j * cb < hk else 1.0
            for h in range(cb // dk):
                cs = c[:, h * dk:(h + 1) * dk]
                cn = cs * (lax.rsqrt(jnp.sum(cs * cs, axis=-1, keepdims=True) + EPS) * gain)
                qkvz_ref[:, j * cb + h * dk:j * cb + (h + 1) * dk] = cn.astype(BF16)
        else:
            qkvz_ref[:, cols] = c.astype(BF16)
    for j in range(3 * hk // cb, 4 * hk // cb):
        cols = slice(j * cb, (j + 1) * cb)
        qkvz_ref[:, cols] = jnp.dot(xn, wm_ref[:, cols], preferred_element_type=F32).astype(BF16)

    ba = jnp.dot(xn, wba_ref[...], preferred_element_type=F32)
    g = -jnp.exp(alog_ref[...]) * _softplus(ba + dtb_ref[...])
    gc = _chunk_cumsum(g, chunk)
    beta = 1.0 / (1.0 + jnp.exp(-ba))
    lane = lax.broadcasted_iota(jnp.int32, ba.shape, 1)
    packed = jnp.where(lane % 4 < 2, gc, beta)
    for p in range(n_heads // 2):
        aux_ref[:, p * LANES:(p + 1) * LANES] = packed if p == 0 else pltpu.roll(packed, LANES - 4 * p, axis=1)


def _gdn_in(x, norm_w, w_in, layer, w_ba, w_conv, alog, dtb, *, batch, tm):
    m, d = x.shape
    nt = m // batch // tm
    hk = GDN_HEADS * GDN_DK
    n_main = 4 * hk
    n_aux = GDN_HEADS // 2 * LANES
    row = lambda i: (_tile_index(i), 0)
    return pl.pallas_call(
        functools.partial(_gdn_in_body, tm=tm, n_heads=GDN_HEADS, dk=GDN_DK, chunk=GDN_CHUNK, cb=512,
                          tiles_per_seq=nt),
        grid=(N_WCHUNK + batch * nt,),
        in_specs=[pl.BlockSpec((tm, d), row), _resident((1, d)), _w_spec(w_in, layer, n_main),
                  _resident((d, LANES)), _resident((GDN_CONV, 3 * hk)), _resident((1, LANES)),
                  _resident((1, LANES))],
        out_specs=[pl.BlockSpec((tm, n_main), row), pl.BlockSpec((tm, n_aux), row)],
        out_shape=[jax.ShapeDtypeStruct((m, n_main), BF16), jax.ShapeDtypeStruct((m, n_aux), F32)],
        scratch_shapes=[pltpu.VMEM((8, 3 * hk), F32), _w_scratch(w_in, n_main)],
        compiler_params=_cparams(("arbitrary",)),
        name="gdn_in",
    )(x, norm_w, w_in, w_ba, w_conv, alog, dtb)


def _bd(y, dk):
    zero = jnp.zeros((y.shape[0], dk), y.dtype)
    top = jnp.concatenate([y[:, :dk], zero], axis=1)
    bot = jnp.concatenate([zero, y[:, dk:]], axis=1)
    return jnp.concatenate([top, bot], axis=0)


def _pair_bcast(col_a, col_b, rows, dk):
    return jnp.concatenate([jnp.broadcast_to(col_a, (rows, dk)),
                            jnp.broadcast_to(col_b, (rows, dk))], axis=1)


def _gdn_chunk_body(q_ref, k_ref, v_ref, z_ref, aux_ref, onw_ref, o_ref, s_ref, *, n_chunks, dk):
    c_len = GDN_CHUNK
    w2 = 2 * dk
    n_batch = q_ref.shape[0]
    units = [(b, c) for b in range(n_batch) for c in range(n_chunks)]

    @pl.when(pl.program_id(1) == 0)
    def _():
        s_ref[...] = jnp.zeros_like(s_ref)

    row = lax.broadcasted_iota(jnp.int32, (c_len, w2), 0)
    col = lax.broadcasted_iota(jnp.int32, (c_len, w2), 1) % dk
    lower = row >= col
    strict = row > col
    eye = jnp.where(row == col, 1.0, 0.0)
    srow = lax.broadcasted_iota(jnp.int32, (w2, w2), 0) // dk
    scol = lax.broadcasted_iota(jnp.int32, (w2, w2), 1) // dk
    same_head = srow == scol
    bd = lambda y: _bd(y.astype(BF16), dk)
    rows_of = lambda c: slice(c * c_len, (c + 1) * c_len)

    q, k, kb, vb, gc, gc_last, e_gc, decay = [], [], [], [], [], [], [], []
    for b, c in units:
        aux = aux_ref[b, rows_of(c), :]
        g = _pair_bcast(aux[:, 0:1], aux[:, 1:2], c_len, dk)
        beta = _pair_bcast(aux[:, 2:3], aux[:, 3:4], c_len, dk)
        g_t = jnp.concatenate([g[:, :dk].T, g[:, dk:].T], axis=1)
        decay.append(jnp.where(lower, jnp.exp(jnp.minimum(g - g_t, 0.0)), 0.0))
        gc.append(g)
        gc_last.append(g[c_len - 1:c_len, :])
        e_gc.append(jnp.exp(g))
        q.append(q_ref[b, rows_of(c), :].astype(F32))
        k.append(k_ref[b, rows_of(c), :].astype(F32))
        kb.append(k[-1] * beta)
        vb.append(v_ref[b, rows_of(c), :].astype(F32) * beta)

    a_qk, lmat = [], []
    for i in range(len(units)):
        qk_kbk = lax.dot_general(jnp.concatenate([q[i], kb[i]], axis=0).astype(BF16), bd(k[i]),
                                 (((1,), (1,)), ((), ())), preferred_element_type=F32)
        a_qk.append(jnp.where(lower, qk_kbk[:c_len] * decay[i], 0.0))
        lmat.append(jnp.where(strict, qk_kbk[c_len:] * decay[i], 0.0))

    lmat = [l.astype(BF16) for l in lmat]
    zero = jnp.zeros((), BF16)
    p1 = [jnp.where(row // 8 == col // 8, l, zero) for l in lmat]
    p2 = [_mm(p, bd(p)).astype(BF16) for p in p1]
    p4 = [_mm(p, bd(p)).astype(BF16) for p in p2]
    inv = [(eye - p).astype(BF16) for p in p1]
    inv = [(x + _mm(x, bd(p))).astype(BF16) for x, p in zip(inv, p2)]
    inv = [(x + _mm(x, bd(p))).astype(BF16) for x, p in zip(inv, p4)]
    size = 8
    while size < c_len:
        sel = (row // (2 * size) == col // (2 * size)) & (row // size != col // size)
        t = [_mm(jnp.where(sel, l, zero), bd(x)).astype(BF16) for l, x in zip(lmat, inv)]
        inv = [(x - _mm(x, bd(y))).astype(BF16) for x, y in zip(inv, t)]
        size *= 2

    u = [_mm(x, bd(y)) for x, y in zip(inv, vb)]
    w = [_mm(x, bd(kb[i] * e_gc[i])) for i, x in enumerate(inv)]
    wq = [jnp.concatenate([w[i], q[i] * e_gc[i]], axis=0).astype(BF16) for i in range(len(units))]
    k_dec = [(k[i] * jnp.exp(gc_last[i] - gc[i])).astype(BF16) for i in range(len(units))]

    state = [s_ref[b] for b in range(n_batch)]
    for c in range(n_chunks):
        idx = [b * n_chunks + c for b in range(n_batch)]
        wq_s = [_mm(wq[i], state[b]) for b, i in enumerate(idx)]
        v_new = [u[i] - wq_s[b][:c_len] for b, i in enumerate(idx)]
        kv = [lax.dot_general(k_dec[i], v_new[b].astype(BF16), (((0,), (0,)), ((), ())),
                              preferred_element_type=F32) for b, i in enumerate(idx)]
        state = [state[b] * jnp.exp(gc_last[i]) + jnp.where(same_head, kv[b], 0.0)
                 for b, i in enumerate(idx)]
        for b, i in enumerate(idx):
            o = wq_s[b][c_len:] + _mm(a_qk[i], bd(v_new[b]))
            ms = _pair_bcast(jnp.mean(o[:, :dk] * o[:, :dk], axis=-1, keepdims=True),
                             jnp.mean(o[:, dk:] * o[:, dk:], axis=-1, keepdims=True), c_len, dk)
            z = z_ref[b, rows_of(c), :].astype(F32)
            o_ref[b, rows_of(c), :] = (o * lax.rsqrt(ms + EPS) * onw_ref[...] * _silu(z)).astype(o_ref.dtype)

    for b in range(n_batch):
        s_ref[b] = state[b]


def _gdn_chunks(qkvz, aux, out_norm2, *, n_chunks):
    batch, seq, _ = qkvz.shape
    dk = GDN_DK
    tb = n_chunks * GDN_CHUNK
    n_pairs = GDN_HEADS // 2
    spec = lambda offset: pl.BlockSpec((batch, tb, 2 * dk), lambda p, j: (0, j, offset + p))
    return pl.pallas_call(
        functools.partial(_gdn_chunk_body, n_chunks=n_chunks, dk=dk),
        grid=(n_pairs, seq // tb),
        in_specs=[spec(0), spec(n_pairs), spec(2 * n_pairs), spec(3 * n_pairs),
                  pl.BlockSpec((batch, tb, LANES), lambda p, j: (0, j, p)), _resident((1, 2 * dk))],
        out_specs=spec(0),
        out_shape=jax.ShapeDtypeStruct((batch, seq, GDN_HEADS * dk), BF16),
        scratch_shapes=[pltpu.VMEM((batch, 2 * dk, 2 * dk), F32)],
        compiler_params=_cparams(("arbitrary", "arbitrary")),
        name="gdn_chunks",
    )(qkvz, qkvz, qkvz, qkvz, aux, out_norm2)


def _swap_halves(x):
    half = x.shape[1] // 2
    return jnp.concatenate([x[:, half:], x[:, :half]], axis=1)


def _swa_body(sink_ref, q_ref, kp_ref, kc_ref, vp_ref, vc_ref, o_ref, *, n_sub):
    blk = SWA_BLOCK
    first = pl.program_id(1) == 0
    qi = lax.broadcasted_iota(jnp.int32, (2 * blk, blk), 0) % blk
    kj = lax.broadcasted_iota(jnp.int32, (2 * blk, blk), 1)
    from_prev = kj > qi
    top = lax.broadcasted_iota(jnp.int32, (2 * blk, 1), 0) < blk
    lane = lax.broadcasted_iota(jnp.int32, (blk, LANES), 1)
    low_half = lane < SWA_HD
    scale = SWA_HD ** -0.5

    for slab in range(SWA_KV_HEADS // 2):
        cols = slice(slab * LANES, (slab + 1) * LANES)
        k_all = jnp.concatenate([kp_ref[:, cols], kc_ref[:, cols]], axis=0)
        v_all = jnp.concatenate([vp_ref[:, cols], vc_ref[:, cols]], axis=0)
        k_sw = _swap_halves(k_all)
        v_sw = _swap_halves(v_all)
        for sub in range(n_sub):
            rows = slice(sub * blk, (sub + 1) * blk)
            keys = slice(sub * blk, (sub + 2) * blk)
            no_prev = first if sub == 0 else None
            jobs = []
            for kv_half in range(2):
                qslab0 = 2 * (2 * slab + kv_half)
                for half in range(2):
                    keep = low_half if half == 0 else jnp.logical_not(low_half)
                    qm = [jnp.where(keep, q_ref[rows, (qslab0 + t) * LANES:(qslab0 + t + 1) * LANES] * scale, 0.0)
                          for t in range(2)]
                    kmat = (k_all if half == kv_half else k_sw)[keys]
                    s = lax.dot_general(jnp.concatenate(qm, axis=0).astype(BF16), kmat,
                                        (((1,), (1,)), ((), ())), preferred_element_type=F32)
                    jobs.append((kv_half, half, qslab0, s))
            outs = {}
            for kv_half, half, qslab0, s in jobs:
                s_prev = s[:, :blk]
                if no_prev is not None:
                    s_prev = jnp.where(no_prev, -jnp.inf, s_prev)
                sf = jnp.where(from_prev, s_prev, s[:, blk:])
                sink = jnp.where(top, sink_ref[2 * qslab0 + half], sink_ref[2 * qslab0 + 2 + half])
                mx = jnp.maximum(jnp.max(sf, axis=-1, keepdims=True), sink)
                p = jnp.exp(sf - mx)
                denom = jnp.sum(p, axis=-1, keepdims=True) + jnp.exp(sink - mx)
                p2 = jnp.concatenate([jnp.where(from_prev, p, 0.0), jnp.where(from_prev, 0.0, p)], axis=1)
                vmat = (v_all if half == kv_half else v_sw)[keys]
                outs[(qslab0, half)] = jnp.dot(p2.astype(BF16), vmat, preferred_element_type=F32) / denom
            for kv_half in range(2):
                qslab0 = 2 * (2 * slab + kv_half)
                for t in range(2):
                    part = slice(t * blk, (t + 1) * blk)
                    o_ref[rows, (qslab0 + t) * LANES:(qslab0 + t + 1) * LANES] = jnp.where(
                        low_half, outs[(qslab0, 0)][part], outs[(qslab0, 1)][part]).astype(o_ref.dtype)


def _swa(qkv, sinks, *, batch, n_sub):
    m = qkv.shape[0]
    blk = SWA_BLOCK
    rows = n_sub * blk
    nt = m // batch // rows
    hq = SWA_HEADS * SWA_HD
    hkv = SWA_KV_HEADS * SWA_HD
    cur = lambda col: (lambda b, n: (b * nt + n, col))
    prev = lambda col: (lambda b, n: ((b * nt + n) * n_sub - jnp.minimum(n, 1), col))
    return pl.pallas_call(
        functools.partial(_swa_body, n_sub=n_sub),
        grid=(batch, nt),
        in_specs=[pl.BlockSpec(memory_space=pltpu.SMEM),
                  pl.BlockSpec((rows, hq), cur(0)),
                  pl.BlockSpec((blk, hkv), prev(hq // hkv)),
                  pl.BlockSpec((rows, hkv), cur(hq // hkv)),
                  pl.BlockSpec((blk, hkv), prev(hq // hkv + 1)),
                  pl.BlockSpec((rows, hkv), cur(hq // hkv + 1))],
        out_specs=pl.BlockSpec((rows, hq), cur(0)),
        out_shape=jax.ShapeDtypeStruct((m, hq), BF16),
        compiler_params=_cparams(("arbitrary", "arbitrary")),
        name="swa",
    )(sinks, qkv, qkv, qkv, qkv, qkv)


def _pair_lanes(vec_a, vec_b):
    n_pairs = vec_a.shape[0] // 2
    quad = jnp.concatenate([vec_a.reshape(n_pairs, 2), vec_b.reshape(n_pairs, 2)], axis=1)
    return jnp.pad(quad.reshape(1, 4 * n_pairs), ((0, 0), (0, LANES - 4 * n_pairs)))


def _pair_cols(w_a, w_b):
    d, h = w_a.shape
    quad = jnp.concatenate([w_a.reshape(d, h // 2, 2), w_b.reshape(d, h // 2, 2)], axis=2)
    return jnp.pad(quad.reshape(d, 2 * h), ((0, 0), (0, LANES - 2 * h)))


def kernel(x, ffn1_norm, ffn1_w_gu, ffn1_w_down, mix_norm, ffn2_norm, ffn2_w_gu, ffn2_w_down,
           a_w_in, a_w_conv, a_A_log, a_dt_bias, a_out_norm, a_w_out,
           b_w_in, b_b_in, b_sinks, b_w_out, b_b_out, final_norm):
    batch, seq, d = x.shape
    depth = ffn1_norm.shape[0]
    m = batch * seq
    tm = min(512, seq)
    xf = x.reshape(m, d)
    row = lambda v: v.reshape(1, -1).astype(F32)

    for layer in range(depth):
        j = layer // 2
        last = layer == depth - 1
        ffn1 = (row(ffn1_norm[layer]), ffn1_w_gu, ffn1_w_down, layer)
        ffn2 = (row(ffn2_norm[layer]), ffn2_w_gu, ffn2_w_down, layer)
        if layer % 2 == 0:
            xf = _ffn(xf, *ffn1, tm=tm)
            hk = GDN_HEADS * GDN_DK
            w_b = a_w_in[j, :, 4 * hk:4 * hk + GDN_HEADS]
            w_a = a_w_in[j, :, 4 * hk + GDN_HEADS:]
            zeros_h = jnp.zeros((GDN_HEADS,), F32)
            qkvz, aux = _gdn_in(xf, row(mix_norm[layer]), a_w_in, j, _pair_cols(w_a, w_b).astype(BF16),
                                a_w_conv[j].astype(F32), _pair_lanes(a_A_log[j].astype(F32), zeros_h),
                                _pair_lanes(a_dt_bias[j].astype(F32), zeros_h), batch=batch, tm=tm)
            o = _gdn_chunks(qkvz.reshape(batch, seq, -1), aux.reshape(batch, seq, -1),
                            jnp.tile(row(a_out_norm[j]), (1, 2)), n_chunks=min(8, seq // GDN_CHUNK))
            pre = (o.reshape(m, -1), a_w_out, j, jnp.zeros((1, d), F32))
        else:
            xf, qkv = _ffn(xf, *ffn1, tm=tm, proj=(row(mix_norm[layer]), b_w_in, j, row(b_b_in[j])))
            o = _swa(qkv, b_sinks[j].astype(F32), batch=batch, n_sub=min(4, seq // SWA_BLOCK))
            pre = (o, b_w_out, j, row(b_b_out[j]))
        xf = _ffn(xf, *ffn2, tm=tm, pre=pre, final_w=row(final_norm) if last else None)
    return xf.reshape(batch, seq, d)
```

```python
import functools

import jax
import jax.numpy as jnp
from jax import lax
from jax.experimental import pallas as pl
from jax.experimental.pallas import tpu as pltpu

F32 = jnp.float32
BF16 = jnp.bfloat16
EPS = 1e-6

LANES = 128
VMEM_LIMIT_BYTES = 62 * 2**20

GDN_HEADS = 8
GDN_DK = 128
GDN_CONV = 4
GDN_CHUNK = 128
SWA_HEADS = 16
SWA_KV_HEADS = 4
SWA_HD = 64
SWA_BLOCK = 128


def _cparams(semantics):
    return pltpu.CompilerParams(dimension_semantics=semantics, vmem_limit_bytes=VMEM_LIMIT_BYTES)


def _resident(shape):
    return pl.BlockSpec(shape, lambda *_: (0,) * len(shape), pipeline_mode=pl.Buffered(1))


def _rms(x, w):
    ms = jnp.mean(x * x, axis=-1, keepdims=True)
    return x * lax.rsqrt(ms + EPS) * w


def _silu(x):
    return x / (1.0 + jnp.exp(-x))


def _mm(a, b):
    return jnp.dot(a.astype(BF16), b.astype(BF16), preferred_element_type=F32)


N_WCHUNK = 16


def _w_spec(w, layer):
    _, n_rows, n_cols = w.shape
    return pl.BlockSpec((None, n_rows // N_WCHUNK, n_cols),
                        lambda i: (layer, jnp.minimum(i, N_WCHUNK - 1), 0))


def _w_scratch(w, n_cols=None):
    return pltpu.VMEM((w.shape[1], n_cols or w.shape[2]), BF16)


def _cast_in(i, pairs):
    @pl.when(i < N_WCHUNK)
    def _():
        for src, dst in pairs:
            rows = src.shape[0]
            dst[pl.ds(pl.multiple_of(i * rows, rows), rows), :] = src[:, :dst.shape[1]].astype(BF16)


def _tile_index(i):
    return jnp.maximum(i - N_WCHUNK, 0)


def _ffn_body(*refs, d_ff, has_pre, tail):
    i = pl.program_id(0)
    refs = list(refs)
    n_w = 2 + has_pre + (tail == "proj")
    w_s = refs[-n_w:]
    refs = refs[:-n_w]
    x_ref = refs.pop(0)
    casts = []
    if has_pre:
        o_ref, wo_ref, bo_ref = refs.pop(0), refs.pop(0), refs.pop(0)
        wo_s = w_s.pop(0)
        casts.append((wo_ref, wo_s))
    nw_ref, wgu_ref, wd_ref = refs.pop(0), refs.pop(0), refs.pop(0)
    wgu_s, wd_s = w_s.pop(0), w_s.pop(0)
    casts += [(wgu_ref, wgu_s), (wd_ref, wd_s)]
    if tail == "proj":
        nw2_ref, wp_ref, bp_ref, y_ref, p_ref = refs
        wp_s = w_s.pop(0)
        casts.append((wp_ref, wp_s))
    elif tail == "norm":
        fw_ref, y_ref = refs
    else:
        (y_ref,) = refs
    _cast_in(i, casts)

    @pl.when(i >= N_WCHUNK)
    def _():
        x = x_ref[...]
        if has_pre:
            x = x + jnp.dot(o_ref[...], wo_s[...], preferred_element_type=F32) + bo_ref[...]
        xn = _rms(x, nw_ref[...]).astype(BF16)
        gu = jnp.dot(xn, wgu_s[...], preferred_element_type=F32)
        h = (_silu(gu[:, :d_ff]) * gu[:, d_ff:]).astype(BF16)
        out = x + 0.5 * jnp.dot(h, wd_s[...], preferred_element_type=F32)
        if tail == "norm":
            y_ref[...] = _rms(out, fw_ref[...])
        else:
            y_ref[...] = out
        if tail == "proj":
            hn = _rms(out, nw2_ref[...]).astype(BF16)
            p_ref[...] = (jnp.dot(hn, wp_s[...], preferred_element_type=F32) + bp_ref[...]).astype(p_ref.dtype)


def _ffn(x, norm_w, w_gu, w_down, layer, *, tm, pre=None, final_w=None, proj=None):
    m, d = x.shape
    d_ff = w_down.shape[1]
    row = lambda n: pl.BlockSpec((tm, n), lambda i: (_tile_index(i), 0))
    args, specs, scratch = [x], [row(d)], []
    if pre is not None:
        o, w_o, j_o, b_o = pre
        args += [o, w_o, b_o]
        specs += [row(o.shape[1]), _w_spec(w_o, j_o), _resident((1, d))]
        scratch.append(_w_scratch(w_o))
    args += [norm_w, w_gu, w_down]
    specs += [_resident((1, d)), _w_spec(w_gu, layer), _w_spec(w_down, layer)]
    scratch += [_w_scratch(w_gu), _w_scratch(w_down)]
    out_specs, out_shape = row(d), jax.ShapeDtypeStruct((m, d), F32)
    tail = "plain"
    if final_w is not None:
        tail = "norm"
        args.append(final_w)
        specs.append(_resident((1, d)))
    elif proj is not None:
        tail = "proj"
        nw2, w_p, j_p, b_p = proj
        n_p = w_p.shape[2]
        args += [nw2, w_p, b_p]
        specs += [_resident((1, d)), _w_spec(w_p, j_p), _resident((1, n_p))]
        scratch.append(_w_scratch(w_p))
        out_specs = [row(d), row(n_p)]
        out_shape = [out_shape, jax.ShapeDtypeStruct((m, n_p), BF16)]
    return pl.pallas_call(
        functools.partial(_ffn_body, d_ff=d_ff, has_pre=pre is not None, tail=tail),
        grid=(N_WCHUNK + m // tm,),
        in_specs=specs,
        out_specs=out_specs,
        out_shape=out_shape,
        scratch_shapes=scratch,
        compiler_params=_cparams(("arbitrary",)),
        name="ffn",
    )(*args)


def _chunk_cumsum(g, chunk):
    pos = lax.broadcasted_iota(jnp.int32, g.shape, 0) % chunk
    shift = 1
    while shift < chunk:
        g = g + jnp.where(pos >= shift, pltpu.roll(g, shift, axis=0), 0.0)
        shift *= 2
    return g


def _softplus(x):
    return jnp.maximum(x, 0.0) + jnp.log(1.0 + jnp.exp(-jnp.abs(x)))


def _gdn_in_body(x_ref, nw_ref, wm_ref, wba_ref, wc_ref, alog_ref, dtb_ref,
                 qkvz_ref, aux_ref, carry, wm_s, *, tm, n_heads, dk, chunk, cb, tiles_per_seq):
    i = pl.program_id(0)
    _cast_in(i, [(wm_ref, wm_s)])

    @pl.when(i >= N_WCHUNK)
    def _():
        _gdn_in_tile(x_ref, nw_ref, wm_s, wba_ref, wc_ref, alog_ref, dtb_ref, qkvz_ref, aux_ref, carry,
                     first=(i - N_WCHUNK) % tiles_per_seq == 0, tm=tm, n_heads=n_heads, dk=dk,
                     chunk=chunk, cb=cb)


def _gdn_in_tile(x_ref, nw_ref, wm_ref, wba_ref, wc_ref, alog_ref, dtb_ref,
                 qkvz_ref, aux_ref, carry, *, first, tm, n_heads, dk, chunk, cb):
    hk = n_heads * dk
    halo = carry.shape[0]
    xn = _rms(x_ref[...], nw_ref[...]).astype(BF16)

    @pl.when(first)
    def _():
        carry[...] = jnp.zeros_like(carry)

    for j in range(3 * hk // cb):
        cols = slice(j * cb, (j + 1) * cb)
        pre = jnp.dot(xn, wm_ref[:, cols], preferred_element_type=F32)
        ext = jnp.concatenate([carry[:, cols], pre], axis=0)
        carry[:, cols] = pre[tm - halo:, :]
        wc = wc_ref[:, cols]
        ext1 = pltpu.roll(ext, 1, axis=0)
        lo = wc[1:2, :] * ext + wc[0:1, :] * ext1
        hi = wc[3:4, :] * ext + wc[2:3, :] * ext1
        acc = (hi + pltpu.roll(lo, 2, axis=0))[halo:, :]
        c = _silu(acc)
        if j * cb < 2 * hk:
            gain = dk ** -0.5 if j * cb < hk else 1.0
            for h in range(cb // dk):
                cs = c[:, h * dk:(h + 1) * dk]
                cn = cs * (lax.rsqrt(jnp.sum(cs * cs, axis=-1, keepdims=True) + EPS) * gain)
                qkvz_ref[:, j * cb + h * dk:j * cb + (h + 1) * dk] = cn.astype(BF16)
        else:
            qkvz_ref[:, cols] = c.astype(BF16)
    for j in range(3 * hk // cb, 4 * hk // cb):
        cols = slice(j * cb, (j + 1) * cb)
        qkvz_ref[:, cols] = jnp.dot(xn, wm_ref[:, cols], preferred_element_type=F32).astype(BF16)

    ba = jnp.dot(xn, wba_ref[...], preferred_element_type=F32)
    g = -jnp.exp(alog_ref[...]) * _softplus(ba + dtb_ref[...])
    gc = _chunk_cumsum(g, chunk)
    beta = 1.0 / (1.0 + jnp.exp(-ba))
    lane = lax.broadcasted_iota(jnp.int32, ba.shape, 1)
    packed = jnp.where(lane % 4 < 2, gc, beta)
    for p in range(n_heads // 2):
        aux_ref[:, p * LANES:(p + 1) * LANES] = packed if p == 0 else pltpu.roll(packed, LANES - 4 * p, axis=1)


def _gdn_in(x, norm_w, w_in, layer, w_ba, w_conv, alog, dtb, *, batch, tm):
    m, d = x.shape
    nt = m // batch // tm
    hk = GDN_HEADS * GDN_DK
    n_main = 4 * hk
    n_aux = GDN_HEADS // 2 * LANES
    row = lambda i: (_tile_index(i), 0)
    return pl.pallas_call(
        functools.partial(_gdn_in_body, tm=tm, n_heads=GDN_HEADS, dk=GDN_DK, chunk=GDN_CHUNK, cb=512,
                          tiles_per_seq=nt),
        grid=(N_WCHUNK + batch * nt,),
        in_specs=[pl.BlockSpec((tm, d), row), _resident((1, d)), _w_spec(w_in, layer),
                  _resident((d, LANES)), _resident((GDN_CONV, 3 * hk)), _resident((1, LANES)),
                  _resident((1, LANES))],
        out_specs=[pl.BlockSpec((tm, n_main), row), pl.BlockSpec((tm, n_aux), row)],
        out_shape=[jax.ShapeDtypeStruct((m, n_main), BF16), jax.ShapeDtypeStruct((m, n_aux), F32)],
        scratch_shapes=[pltpu.VMEM((8, 3 * hk), F32), _w_scratch(w_in, n_main)],
        compiler_params=_cparams(("arbitrary",)),
        name="gdn_in",
    )(x, norm_w, w_in, w_ba, w_conv, alog, dtb)


def _bd(y, dk):
    zero = jnp.zeros((y.shape[0], dk), y.dtype)
    top = jnp.concatenate([y[:, :dk], zero], axis=1)
    bot = jnp.concatenate([zero, y[:, dk:]], axis=1)
    return jnp.concatenate([top, bot], axis=0)


def _pair_bcast(col_a, col_b, rows, dk):
    return jnp.concatenate([jnp.broadcast_to(col_a, (rows, dk)),
                            jnp.broadcast_to(col_b, (rows, dk))], axis=1)


def _gdn_chunk_body(q_ref, k_ref, v_ref, z_ref, aux_ref, onw_ref, o_ref, s_ref, *, n_chunks, dk):
    c_len = GDN_CHUNK
    w2 = 2 * dk
    n_batch = q_ref.shape[0]
    units = [(b, c) for b in range(n_batch) for c in range(n_chunks)]

    @pl.when(pl.program_id(1) == 0)
    def _():
        s_ref[...] = jnp.zeros_like(s_ref)

    row = lax.broadcasted_iota(jnp.int32, (c_len, w2), 0)
    col = lax.broadcasted_iota(jnp.int32, (c_len, w2), 1) % dk
    lower = row >= col
    strict = row > col
    eye = jnp.where(row == col, 1.0, 0.0)
    srow = lax.broadcasted_iota(jnp.int32, (w2, w2), 0) // dk
    scol = lax.broadcasted_iota(jnp.int32, (w2, w2), 1) // dk
    same_head = srow == scol
    bd = lambda y: _bd(y.astype(BF16), dk)
    rows_of = lambda c: slice(c * c_len, (c + 1) * c_len)

    q, k, kb, vb, gc, gc_last, e_gc, decay = [], [], [], [], [], [], [], []
    for b, c in units:
        aux = aux_ref[b, rows_of(c), :]
        g = _pair_bcast(aux[:, 0:1], aux[:, 1:2], c_len, dk)
        beta = _pair_bcast(aux[:, 2:3], aux[:, 3:4], c_len, dk)
        g_t = jnp.concatenate([g[:, :dk].T, g[:, dk:].T], axis=1)
        decay.append(jnp.where(lower, jnp.exp(jnp.minimum(g - g_t, 0.0)), 0.0))
        gc.append(g)
        gc_last.append(g[c_len - 1:c_len, :])
        e_gc.append(jnp.exp(g))
        q.append(q_ref[b, rows_of(c), :].astype(F32))
        k.append(k_ref[b, rows_of(c), :].astype(F32))
        kb.append(k[-1] * beta)
        vb.append(v_ref[b, rows_of(c), :].astype(F32) * beta)

    a_qk, lmat = [], []
    for i in range(len(units)):
        qk_kbk = lax.dot_general(jnp.concatenate([q[i], kb[i]], axis=0).astype(BF16), bd(k[i]),
                                 (((1,), (1,)), ((), ())), preferred_element_type=F32)
        a_qk.append(jnp.where(lower, qk_kbk[:c_len] * decay[i], 0.0))
        lmat.append(jnp.where(strict, qk_kbk[c_len:] * decay[i], 0.0))

    lmat = [l.astype(BF16) for l in lmat]
    zero = jnp.zeros((), BF16)
    p1 = [jnp.where(row // 8 == col // 8, l, zero) for l in lmat]
    p2 = [_mm(p, bd(p)).astype(BF16) for p in p1]
    p4 = [_mm(p, bd(p)).astype(BF16) for p in p2]
    inv = [(eye - p).astype(BF16) for p in p1]
    inv = [(x + _mm(x, bd(p))).astype(BF16) for x, p in zip(inv, p2)]
    inv = [(x + _mm(x, bd(p))).astype(BF16) for x, p in zip(inv, p4)]
    size = 8
    while size < c_len:
        sel = (row // (2 * size) == col // (2 * size)) & (row // size != col // size)
        t = [_mm(jnp.where(sel, l, zero), bd(x)).astype(BF16) for l, x in zip(lmat, inv)]
        inv = [(x - _mm(x, bd(y))).astype(BF16) for x, y in zip(inv, t)]
        size *= 2

    u = [_mm(x, bd(y)) for x, y in zip(inv, vb)]
    w = [_mm(x, bd(kb[i] * e_gc[i])) for i, x in enumerate(inv)]
    wq = [jnp.concatenate([w[i], q[i] * e_gc[i]], axis=0).astype(BF16) for i in range(len(units))]
    k_dec = [(k[i] * jnp.exp(gc_last[i] - gc[i])).astype(BF16) for i in range(len(units))]

    state = [s_ref[b] for b in range(n_batch)]
    for c in range(n_chunks):
        idx = [b * n_chunks + c for b in range(n_batch)]
        wq_s = [_mm(wq[i], state[b]) for b, i in enumerate(idx)]
        v_new = [u[i] - wq_s[b][:c_len] for b, i in enumerate(idx)]
        kv = [lax.dot_general(k_dec[i], v_new[b].astype(BF16), (((0,), (0,)), ((), ())),
                              preferred_element_type=F32) for b, i in enumerate(idx)]
        state = [state[b] * jnp.exp(gc_last[i]) + jnp.where(same_head, kv[b], 0.0)
                 for b, i in enumerate(idx)]
        for b, i in enumerate(idx):
            o = wq_s[b][c_len:] + _mm(a_qk[i], bd(v_new[b]))
            ms = _pair_bcast(jnp.mean(o[:, :dk] * o[:, :dk], axis=-1, keepdims=True),
                             jnp.mean(o[:, dk:] * o[:, dk:], axis=-1, keepdims=True), c_len, dk)
            z = z_ref[b, rows_of(c), :].astype(F32)
            o_ref[b, rows_of(c), :] = (o * lax.rsqrt(ms + EPS) * onw_ref[...] * _silu(z)).astype(o_ref.dtype)

    for b in range(n_batch):
        s_ref[b] = state[b]


def _gdn_chunks(qkvz, aux, out_norm2, *, n_chunks):
    batch, seq, _ = qkvz.shape
    dk = GDN_DK
    tb = n_chunks * GDN_CHUNK
    n_pairs = GDN_HEADS // 2
    spec = lambda offset: pl.BlockSpec((batch, tb, 2 * dk), lambda p, j: (0, j, offset + p))
    return pl.pallas_call(
        functools.partial(_gdn_chunk_body, n_chunks=n_chunks, dk=dk),
        grid=(n_pairs, seq // tb),
        in_specs=[spec(0), spec(n_pairs), spec(2 * n_pairs), spec(3 * n_pairs),
                  pl.BlockSpec((batch, tb, LANES), lambda p, j: (0, j, p)), _resident((1, 2 * dk))],
        out_specs=spec(0),
        out_shape=jax.ShapeDtypeStruct((batch, seq, GDN_HEADS * dk), BF16),
        scratch_shapes=[pltpu.VMEM((batch, 2 * dk, 2 * dk), F32)],
        compiler_params=_cparams(("arbitrary", "arbitrary")),
        name="gdn_chunks",
    )(qkvz, qkvz, qkvz, qkvz, aux, out_norm2)


def _swap_halves(x):
    half = x.shape[1] // 2
    return jnp.concatenate([x[:, half:], x[:, :half]], axis=1)


def _swa_body(sink_ref, q_ref, kp_ref, kc_ref, vp_ref, vc_ref, o_ref, *, n_sub):
    blk = SWA_BLOCK
    first = pl.program_id(1) == 0
    qi = lax.broadcasted_iota(jnp.int32, (2 * blk, blk), 0) % blk
    kj = lax.broadcasted_iota(jnp.int32, (2 * blk, blk), 1)
    from_prev = kj > qi
    top = lax.broadcasted_iota(jnp.int32, (2 * blk, 1), 0) < blk
    lane = lax.broadcasted_iota(jnp.int32, (blk, LANES), 1)
    low_half = lane < SWA_HD
    scale = SWA_HD ** -0.5

    for slab in range(SWA_KV_HEADS // 2):
        cols = slice(slab * LANES, (slab + 1) * LANES)
        k_all = jnp.concatenate([kp_ref[:, cols], kc_ref[:, cols]], axis=0)
        v_all = jnp.concatenate([vp_ref[:, cols], vc_ref[:, cols]], axis=0)
        k_sw = _swap_halves(k_all)
        v_sw = _swap_halves(v_all)
        for sub in range(n_sub):
            rows = slice(sub * blk, (sub + 1) * blk)
            keys = slice(sub * blk, (sub + 2) * blk)
            no_prev = first if sub == 0 else None
            jobs = []
            for kv_half in range(2):
                qslab0 = 2 * (2 * slab + kv_half)
                for half in range(2):
                    keep = low_half if half == 0 else jnp.logical_not(low_half)
                    qm = [jnp.where(keep, q_ref[rows, (qslab0 + t) * LANES:(qslab0 + t + 1) * LANES] * scale, 0.0)
                          for t in range(2)]
                    kmat = (k_all if half == kv_half else k_sw)[keys]
                    s = lax.dot_general(jnp.concatenate(qm, axis=0).astype(BF16), kmat,
                                        (((1,), (1,)), ((), ())), preferred_element_type=F32)
                    jobs.append((kv_half, half, qslab0, s))
            outs = {}
            for kv_half, half, qslab0, s in jobs:
                s_prev = s[:, :blk]
                if no_prev is not None:
                    s_prev = jnp.where(no_prev, -jnp.inf, s_prev)
                sf = jnp.where(from_prev, s_prev, s[:, blk:])
                sink = jnp.where(top, sink_ref[2 * qslab0 + half], sink_ref[2 * qslab0 + 2 + half])
                mx = jnp.maximum(jnp.max(sf, axis=-1, keepdims=True), sink)
                p = jnp.exp(sf - mx)
                denom = jnp.sum(p, axis=-1, keepdims=True) + jnp.exp(sink - mx)
                p2 = jnp.concatenate([jnp.where(from_prev, p, 0.0), jnp.where(from_prev, 0.0, p)], axis=1)
                vmat = (v_all if half == kv_half else v_sw)[keys]
                outs[(qslab0, half)] = jnp.dot(p2.astype(BF16), vmat, preferred_element_type=F32) / denom
            for kv_half in range(2):
                qslab0 = 2 * (2 * slab + kv_half)
                for t in range(2):
                    part = slice(t * blk, (t + 1) * blk)
                    o_ref[rows, (qslab0 + t) * LANES:(qslab0 + t + 1) * LANES] = jnp.where(
                        low_half, outs[(qslab0, 0)][part], outs[(qslab0, 1)][part]).astype(o_ref.dtype)


def _swa(qkv, sinks, *, batch, n_sub):
    m = qkv.shape[0]
    blk = SWA_BLOCK
    rows = n_sub * blk
    nt = m // batch // rows
    hq = SWA_HEADS * SWA_HD
    hkv = SWA_KV_HEADS * SWA_HD
    cur = lambda col: (lambda b, n: (b * nt + n, col))
    prev = lambda col: (lambda b, n: ((b * nt + n) * n_sub - jnp.minimum(n, 1), col))
    return pl.pallas_call(
        functools.partial(_swa_body, n_sub=n_sub),
        grid=(batch, nt),
        in_specs=[pl.BlockSpec(memory_space=pltpu.SMEM),
                  pl.BlockSpec((rows, hq), cur(0)),
                  pl.BlockSpec((blk, hkv), prev(hq // hkv)),
                  pl.BlockSpec((rows, hkv), cur(hq // hkv)),
                  pl.BlockSpec((blk, hkv), prev(hq // hkv + 1)),
                  pl.BlockSpec((rows, hkv), cur(hq // hkv + 1))],
        out_specs=pl.BlockSpec((rows, hq), cur(0)),
        out_shape=jax.ShapeDtypeStruct((m, hq), BF16),
        compiler_params=_cparams(("arbitrary", "arbitrary")),
        name="swa",
    )(sinks, qkv, qkv, qkv, qkv, qkv)


def _pair_lanes(vec_a, vec_b):
    n_pairs = vec_a.shape[0] // 2
    quad = jnp.concatenate([vec_a.reshape(n_pairs, 2), vec_b.reshape(n_pairs, 2)], axis=1)
    return jnp.pad(quad.reshape(1, 4 * n_pairs), ((0, 0), (0, LANES - 4 * n_pairs)))


def _pair_cols(w_a, w_b):
    d, h = w_a.shape
    quad = jnp.concatenate([w_a.reshape(d, h // 2, 2), w_b.reshape(d, h // 2, 2)], axis=2)
    return jnp.pad(quad.reshape(d, 2 * h), ((0, 0), (0, LANES - 2 * h)))


def kernel(x, ffn1_norm, ffn1_w_gu, ffn1_w_down, mix_norm, ffn2_norm, ffn2_w_gu, ffn2_w_down,
           a_w_in, a_w_conv, a_A_log, a_dt_bias, a_out_norm, a_w_out,
           b_w_in, b_b_in, b_sinks, b_w_out, b_b_out, final_norm):
    batch, seq, d = x.shape
    depth = ffn1_norm.shape[0]
    m = batch * seq
    tm = min(512, seq)
    tm_ffn = min(1024, seq)
    xf = x.reshape(m, d)
    row = lambda v: v.reshape(1, -1).astype(F32)

    for layer in range(depth):
        j = layer // 2
        last = layer == depth - 1
        ffn1 = (row(ffn1_norm[layer]), ffn1_w_gu, ffn1_w_down, layer)
        ffn2 = (row(ffn2_norm[layer]), ffn2_w_gu, ffn2_w_down, layer)
        if layer % 2 == 0:
            xf = _ffn(xf, *ffn1, tm=tm_ffn)
            hk = GDN_HEADS * GDN_DK
            w_b = a_w_in[j, :, 4 * hk:4 * hk + GDN_HEADS]
            w_a = a_w_in[j, :, 4 * hk + GDN_HEADS:]
            zeros_h = jnp.zeros((GDN_HEADS,), F32)
            qkvz, aux = _gdn_in(xf, row(mix_norm[layer]), a_w_in, j, _pair_cols(w_a, w_b).astype(BF16),
                                a_w_conv[j].astype(F32), _pair_lanes(a_A_log[j].astype(F32), zeros_h),
                                _pair_lanes(a_dt_bias[j].astype(F32), zeros_h), batch=batch, tm=tm)
            o = _gdn_chunks(qkvz.reshape(batch, seq, -1), aux.reshape(batch, seq, -1),
                            jnp.tile(row(a_out_norm[j]), (1, 2)), n_chunks=min(8, seq // GDN_CHUNK))
            pre = (o.reshape(m, -1), a_w_out, j, jnp.zeros((1, d), F32))
        else:
            xf, qkv = _ffn(xf, *ffn1, tm=tm_ffn, proj=(row(mix_norm[layer]), b_w_in, j, row(b_b_in[j])))
            o = _swa(qkv, b_sinks[j].astype(F32), batch=batch, n_sub=min(8, seq // SWA_BLOCK))
            pre = (o, b_w_out, j, row(b_b_out[j]))
        xf = _ffn(xf, *ffn2, tm=tm_ffn, pre=pre, final_w=row(final_norm) if last else None)
    return xf.reshape(batch, seq, d)
```

```python
import functools

import jax
import jax.numpy as jnp
from jax import lax
from jax.experimental import pallas as pl
from jax.experimental.pallas import tpu as pltpu

F32 = jnp.float32
BF16 = jnp.bfloat16
EPS = 1e-6

LANES = 128
VMEM_LIMIT_BYTES = 62 * 2**20

GDN_HEADS = 8
GDN_DK = 128
GDN_CONV = 4
GDN_CHUNK = 128
SWA_HEADS = 16
SWA_KV_HEADS = 4
SWA_HD = 64
SWA_BLOCK = 128


def _cparams(semantics):
    return pltpu.CompilerParams(dimension_semantics=semantics, vmem_limit_bytes=VMEM_LIMIT_BYTES)


def _resident(shape):
    return pl.BlockSpec(shape, lambda *_: (0,) * len(shape), pipeline_mode=pl.Buffered(1))


def _rms(x, w):
    ms = jnp.mean(x * x, axis=-1, keepdims=True)
    return x * lax.rsqrt(ms + EPS) * w


def _silu(x):
    return x / (1.0 + jnp.exp(-x))


def _mm(a, b):
    return jnp.dot(a.astype(BF16), b.astype(BF16), preferred_element_type=F32)


N_WCHUNK = 16


def _w_spec(w, layer, n_rows=None):
    return pl.BlockSpec((None, (n_rows or w.shape[1]) // N_WCHUNK, w.shape[2]),
                        lambda i: (layer, jnp.minimum(i, N_WCHUNK - 1), 0))


def _w_scratch(w, n_rows=None):
    return pltpu.VMEM((n_rows or w.shape[1], w.shape[2]), BF16)


def _cast_in(i, pairs):
    @pl.when(i < N_WCHUNK)
    def _():
        for src, dst in pairs:
            rows = src.shape[0]
            dst[pl.ds(pl.multiple_of(i * rows, rows), rows), :] = src[:, :dst.shape[1]].astype(BF16)


def _tile_index(i):
    return jnp.maximum(i - N_WCHUNK, 0)


def _ffn_body(*refs, d_ff, has_pre, tail):
    i = pl.program_id(0)
    refs = list(refs)
    n_w = 2 + has_pre + (tail == "proj")
    w_s = refs[-n_w:]
    refs = refs[:-n_w]
    x_ref = refs.pop(0)
    casts = []
    if has_pre:
        o_ref, wo_ref, bo_ref = refs.pop(0), refs.pop(0), refs.pop(0)
        wo_s = w_s.pop(0)
        casts.append((wo_ref, wo_s))
    nw_ref, wgu_ref, wd_ref = refs.pop(0), refs.pop(0), refs.pop(0)
    wgu_s, wd_s = w_s.pop(0), w_s.pop(0)
    casts += [(wgu_ref, wgu_s), (wd_ref, wd_s)]
    if tail == "proj":
        nw2_ref, wp_ref, bp_ref, y_ref, p_ref = refs
        wp_s = w_s.pop(0)
        casts.append((wp_ref, wp_s))
    elif tail == "norm":
        fw_ref, y_ref = refs
    else:
        (y_ref,) = refs
    _cast_in(i, casts)

    @pl.when(i >= N_WCHUNK)
    def _():
        x = x_ref[...]
        if has_pre:
            x = x + jnp.dot(o_ref[...], wo_s[...], preferred_element_type=F32) + bo_ref[...]
        xn = _rms(x, nw_ref[...]).astype(BF16)
        gu = jnp.dot(xn, wgu_s[...], preferred_element_type=F32)
        h = (_silu(gu[:, :d_ff]) * gu[:, d_ff:]).astype(BF16)
        out = x + 0.5 * jnp.dot(h, wd_s[...], preferred_element_type=F32)
        if tail == "norm":
            y_ref[...] = _rms(out, fw_ref[...])
        else:
            y_ref[...] = out
        if tail == "proj":
            hn = _rms(out, nw2_ref[...]).astype(BF16)
            p_ref[...] = (jnp.dot(hn, wp_s[...], preferred_element_type=F32) + bp_ref[...]).astype(p_ref.dtype)


def _ffn(x, norm_w, w_gu, w_down, layer, *, tm, pre=None, final_w=None, proj=None):
    m, d = x.shape
    d_ff = w_down.shape[1]
    row = lambda n: pl.BlockSpec((tm, n), lambda i: (_tile_index(i), 0))
    args, specs, scratch = [x], [row(d)], []
    if pre is not None:
        o, w_o, j_o, b_o = pre
        args += [o, w_o, b_o]
        specs += [row(o.shape[1]), _w_spec(w_o, j_o), _resident((1, d))]
        scratch.append(_w_scratch(w_o))
    args += [norm_w, w_gu, w_down]
    specs += [_resident((1, d)), _w_spec(w_gu, layer), _w_spec(w_down, layer)]
    scratch += [_w_scratch(w_gu), _w_scratch(w_down)]
    out_specs, out_shape = row(d), jax.ShapeDtypeStruct((m, d), F32)
    tail = "plain"
    if final_w is not None:
        tail = "norm"
        args.append(final_w)
        specs.append(_resident((1, d)))
    elif proj is not None:
        tail = "proj"
        nw2, w_p, j_p, b_p = proj
        n_p = w_p.shape[2]
        args += [nw2, w_p, b_p]
        specs += [_resident((1, d)), _w_spec(w_p, j_p), _resident((1, n_p))]
        scratch.append(_w_scratch(w_p))
        out_specs = [row(d), row(n_p)]
        out_shape = [out_shape, jax.ShapeDtypeStruct((m, n_p), BF16)]
    return pl.pallas_call(
        functools.partial(_ffn_body, d_ff=d_ff, has_pre=pre is not None, tail=tail),
        grid=(N_WCHUNK + m // tm,),
        in_specs=specs,
        out_specs=out_specs,
        out_shape=out_shape,
        scratch_shapes=scratch,
        compiler_params=_cparams(("arbitrary",)),
        name="ffn",
    )(*args)


def _chunk_cumsum(g, chunk):
    pos = lax.broadcasted_iota(jnp.int32, g.shape, 0) % chunk
    shift = 1
    while shift < chunk:
        g = g + jnp.where(pos >= shift, pltpu.roll(g, shift, axis=0), 0.0)
        shift *= 2
    return g


def _softplus(x):
    return jnp.maximum(x, 0.0) + jnp.log(1.0 + jnp.exp(-jnp.abs(x)))


def _gdn_in_body(x_ref, nw_ref, wm_ref, wba_ref, wc_ref, alog_ref, dtb_ref,
                 qkvz_ref, aux_ref, carry, wm_s, *, tm, n_heads, dk, chunk, cb, tiles_per_seq):
    i = pl.program_id(0)
    _cast_in(i, [(wm_ref, wm_s)])

    @pl.when(i >= N_WCHUNK)
    def _():
        _gdn_in_tile(x_ref, nw_ref, wm_s, wba_ref, wc_ref, alog_ref, dtb_ref, qkvz_ref, aux_ref, carry,
                     first=(i - N_WCHUNK) % tiles_per_seq == 0, tm=tm, n_heads=n_heads, dk=dk,
                     chunk=chunk, cb=cb)


def _gdn_in_tile(x_ref, nw_ref, wm_ref, wba_ref, wc_ref, alog_ref, dtb_ref,
                 qkvz_ref, aux_ref, carry, *, first, tm, n_heads, dk, chunk, cb):
    hk = n_heads * dk
    halo = carry.shape[0]
    xn = _rms(x_ref[...], nw_ref[...]).astype(BF16)
    proj = lambda cols: lax.dot_general(xn, wm_ref[cols, :], (((1,), (1,)), ((), ())),
                                        preferred_element_type=F32)

    for j in range(3 * hk // cb):
        cols = slice(j * cb, (j + 1) * cb)
        pre = proj(cols)
        ext = jnp.concatenate([jnp.where(first, 0.0, carry[:, cols]), pre], axis=0)
        carry[:, cols] = pre[tm - halo:, :]
        wc = wc_ref[:, cols]
        ext1 = pltpu.roll(ext, 1, axis=0)
        lo = wc[1:2, :] * ext + wc[0:1, :] * ext1
        hi = wc[3:4, :] * ext + wc[2:3, :] * ext1
        acc = (hi + pltpu.roll(lo, 2, axis=0))[halo:, :]
        c = _silu(acc)
        if j * cb < 2 * hk:
            gain = dk ** -0.5 if j * cb < hk else 1.0
            for h in range(cb // dk):
                cs = c[:, h * dk:(h + 1) * dk]
                cn = cs * (lax.rsqrt(jnp.sum(cs * cs, axis=-1, keepdims=True) + EPS) * gain)
                qkvz_ref[:, j * cb + h * dk:j * cb + (h + 1) * dk] = cn.astype(BF16)
        else:
            qkvz_ref[:, cols] = c.astype(BF16)
    for j in range(3 * hk // cb, 4 * hk // cb):
        cols = slice(j * cb, (j + 1) * cb)
        qkvz_ref[:, cols] = proj(cols).astype(BF16)

    ba = jnp.dot(xn, wba_ref[...].astype(BF16), preferred_element_type=F32)
    g = -jnp.exp(alog_ref[...]) * _softplus(ba + dtb_ref[...])
    gc = _chunk_cumsum(g, chunk)
    beta = 1.0 / (1.0 + jnp.exp(-ba))
    lane = lax.broadcasted_iota(jnp.int32, ba.shape, 1)
    packed = jnp.where(lane % 4 < 2, gc, beta)
    for p in range(n_heads // 2):
        aux_ref[:, p * LANES:(p + 1) * LANES] = packed if p == 0 else pltpu.roll(packed, LANES - 4 * p, axis=1)


def _gdn_in(x, norm_w, w_in_t, layer, w_ba, w_conv, alog, dtb, *, batch, tm):
    m, d = x.shape
    nt = m // batch // tm
    hk = GDN_HEADS * GDN_DK
    n_main = 4 * hk
    n_aux = GDN_HEADS // 2 * LANES
    row = lambda i: (_tile_index(i), 0)
    return pl.pallas_call(
        functools.partial(_gdn_in_body, tm=tm, n_heads=GDN_HEADS, dk=GDN_DK, chunk=GDN_CHUNK, cb=512,
                          tiles_per_seq=nt),
        grid=(N_WCHUNK + batch * nt,),
        in_specs=[pl.BlockSpec((tm, d), row), _resident((1, d)), _w_spec(w_in_t, layer, n_main),
                  _resident((d, LANES)), _resident((GDN_CONV, 3 * hk)), _resident((1, LANES)),
                  _resident((1, LANES))],
        out_specs=[pl.BlockSpec((tm, n_main), row), pl.BlockSpec((tm, n_aux), row)],
        out_shape=[jax.ShapeDtypeStruct((m, n_main), BF16), jax.ShapeDtypeStruct((m, n_aux), F32)],
        scratch_shapes=[pltpu.VMEM((8, 3 * hk), F32), _w_scratch(w_in_t, n_main)],
        compiler_params=_cparams(("arbitrary",)),
        name="gdn_in",
    )(x, norm_w, w_in_t, w_ba, w_conv, alog, dtb)


def _bd(y, dk):
    zero = jnp.zeros((y.shape[0], dk), y.dtype)
    top = jnp.concatenate([y[:, :dk], zero], axis=1)
    bot = jnp.concatenate([zero, y[:, dk:]], axis=1)
    return jnp.concatenate([top, bot], axis=0)


def _pair_bcast(col_a, col_b, rows, dk):
    return jnp.concatenate([jnp.broadcast_to(col_a, (rows, dk)),
                            jnp.broadcast_to(col_b, (rows, dk))], axis=1)


def _gdn_chunk_body(q_ref, k_ref, v_ref, z_ref, aux_ref, onw_ref, o_ref, s_ref, *, n_chunks, dk):
    c_len = GDN_CHUNK
    w2 = 2 * dk
    n_batch = q_ref.shape[0]
    units = [(b, c) for b in range(n_batch) for c in range(n_chunks)]

    @pl.when(pl.program_id(1) == 0)
    def _():
        s_ref[...] = jnp.zeros_like(s_ref)

    row = lax.broadcasted_iota(jnp.int32, (c_len, w2), 0)
    col = lax.broadcasted_iota(jnp.int32, (c_len, w2), 1) % dk
    lower = row >= col
    strict = row > col
    eye = jnp.where(row == col, 1.0, 0.0)
    srow = lax.broadcasted_iota(jnp.int32, (w2, w2), 0) // dk
    scol = lax.broadcasted_iota(jnp.int32, (w2, w2), 1) // dk
    same_head = srow == scol
    bd = lambda y: _bd(y.astype(BF16), dk)
    rows_of = lambda c: slice(c * c_len, (c + 1) * c_len)

    q, k, kb, vb, gc, gc_last, e_gc, decay = [], [], [], [], [], [], [], []
    for b, c in units:
        aux = aux_ref[b, rows_of(c), :]
        g = _pair_bcast(aux[:, 0:1], aux[:, 1:2], c_len, dk)
        beta = _pair_bcast(aux[:, 2:3], aux[:, 3:4], c_len, dk)
        g_t = jnp.concatenate([g[:, :dk].T, g[:, dk:].T], axis=1)
        decay.append(jnp.where(lower, jnp.exp(jnp.minimum(g - g_t, 0.0)), 0.0))
        gc.append(g)
        gc_last.append(g[c_len - 1:c_len, :])
        e_gc.append(jnp.exp(g))
        q.append(q_ref[b, rows_of(c), :].astype(F32))
        k.append(k_ref[b, rows_of(c), :].astype(F32))
        kb.append(k[-1] * beta)
        vb.append(v_ref[b, rows_of(c), :].astype(F32) * beta)

    a_qk, lmat = [], []
    for i in range(len(units)):
        qk_kbk = lax.dot_general(jnp.concatenate([q[i], kb[i]], axis=0).astype(BF16), bd(k[i]),
                                 (((1,), (1,)), ((), ())), preferred_element_type=F32)
        a_qk.append(jnp.where(lower, qk_kbk[:c_len] * decay[i], 0.0))
        lmat.append(jnp.where(strict, qk_kbk[c_len:] * decay[i], 0.0))

    lmat = [l.astype(BF16) for l in lmat]
    zero = jnp.zeros((), BF16)
    p1 = [jnp.where(row // 8 == col // 8, l, zero) for l in lmat]
    p2 = [_mm(p, bd(p)).astype(BF16) for p in p1]
    p4 = [_mm(p, bd(p)).astype(BF16) for p in p2]
    inv = [(eye - p).astype(BF16) for p in p1]
    inv = [(x + _mm(x, bd(p))).astype(BF16) for x, p in zip(inv, p2)]
    inv = [(x + _mm(x, bd(p))).astype(BF16) for x, p in zip(inv, p4)]
    size = 8
    while size < c_len:
        sel = (row // (2 * size) == col // (2 * size)) & (row // size != col // size)
        t = [_mm(jnp.where(sel, l, zero), bd(x)).astype(BF16) for l, x in zip(lmat, inv)]
        inv = [(x - _mm(x, bd(y))).astype(BF16) for x, y in zip(inv, t)]
        size *= 2

    u = [_mm(x, bd(y)) for x, y in zip(inv, vb)]
    w = [_mm(x, bd(kb[i] * e_gc[i])) for i, x in enumerate(inv)]
    wq = [jnp.concatenate([w[i], q[i] * e_gc[i]], axis=0).astype(BF16) for i in range(len(units))]
    k_dec = [(k[i] * jnp.exp(gc_last[i] - gc[i])).astype(BF16) for i in range(len(units))]

    state = [s_ref[b] for b in range(n_batch)]
    for c in range(n_chunks):
        idx = [b * n_chunks + c for b in range(n_batch)]
        wq_s = [_mm(wq[i], state[b]) for b, i in enumerate(idx)]
        v_new = [u[i] - wq_s[b][:c_len] for b, i in enumerate(idx)]
        kv = [lax.dot_general(k_dec[i], v_new[b].astype(BF16), (((0,), (0,)), ((), ())),
                              preferred_element_type=F32) for b, i in enumerate(idx)]
        state = [state[b] * jnp.exp(gc_last[i]) + jnp.where(same_head, kv[b], 0.0)
                 for b, i in enumerate(idx)]
        for b, i in enumerate(idx):
            o = wq_s[b][c_len:] + _mm(a_qk[i], bd(v_new[b]))
            ms = _pair_bcast(jnp.mean(o[:, :dk] * o[:, :dk], axis=-1, keepdims=True),
                             jnp.mean(o[:, dk:] * o[:, dk:], axis=-1, keepdims=True), c_len, dk)
            z = z_ref[b, rows_of(c), :].astype(F32)
            o_ref[b, rows_of(c), :] = (o * lax.rsqrt(ms + EPS) * onw_ref[...] * _silu(z)).astype(o_ref.dtype)

    for b in range(n_batch):
        s_ref[b] = state[b]


def _gdn_chunks(qkvz, aux, out_norm2, *, n_chunks):
    batch, seq, _ = qkvz.shape
    dk = GDN_DK
    tb = n_chunks * GDN_CHUNK
    n_pairs = GDN_HEADS // 2
    spec = lambda offset: pl.BlockSpec((batch, tb, 2 * dk), lambda p, j: (0, j, offset + p))
    return pl.pallas_call(
        functools.partial(_gdn_chunk_body, n_chunks=n_chunks, dk=dk),
        grid=(n_pairs, seq // tb),
        in_specs=[spec(0), spec(n_pairs), spec(2 * n_pairs), spec(3 * n_pairs),
                  pl.BlockSpec((batch, tb, LANES), lambda p, j: (0, j, p)), _resident((1, 2 * dk))],
        out_specs=spec(0),
        out_shape=jax.ShapeDtypeStruct((batch, seq, GDN_HEADS * dk), BF16),
        scratch_shapes=[pltpu.VMEM((batch, 2 * dk, 2 * dk), F32)],
        compiler_params=_cparams(("arbitrary", "arbitrary")),
        name="gdn_chunks",
    )(qkvz, qkvz, qkvz, qkvz, aux, out_norm2)


def _swap_halves(x):
    half = x.shape[1] // 2
    return jnp.concatenate([x[:, half:], x[:, :half]], axis=1)


def _swa_body(sink_ref, q_ref, kp_ref, kc_ref, vp_ref, vc_ref, o_ref, *, n_sub):
    blk = SWA_BLOCK
    group = SWA_HEADS // SWA_KV_HEADS
    first = pl.program_id(1) == 0
    qi = lax.broadcasted_iota(jnp.int32, (group * blk, blk), 0) % blk
    kj = lax.broadcasted_iota(jnp.int32, (group * blk, blk), 1)
    from_prev = kj > qi
    part = lax.broadcasted_iota(jnp.int32, (group * blk, 1), 0) // blk
    lane = lax.broadcasted_iota(jnp.int32, (blk, LANES), 1)
    low_half = lane < SWA_HD
    scale = SWA_HD ** -0.5

    for slab in range(SWA_KV_HEADS // 2):
        cols = slice(slab * LANES, (slab + 1) * LANES)
        k_all = jnp.concatenate([kp_ref[:, cols], kc_ref[:, cols]], axis=0)
        v_all = jnp.concatenate([vp_ref[:, cols], vc_ref[:, cols]], axis=0)
        k_sw = _swap_halves(k_all)
        v_sw = _swap_halves(v_all)
        low_keys = lax.broadcasted_iota(jnp.int32, k_all.shape, 1) < SWA_HD
        for kv_half in range(2):
            k_dup = jnp.where(low_keys, k_all, k_sw) if kv_half == 0 else jnp.where(low_keys, k_sw, k_all)
            v_dup = jnp.where(low_keys, v_all, v_sw) if kv_half == 0 else jnp.where(low_keys, v_sw, v_all)
            qslab0 = 2 * (2 * slab + kv_half)
            heads = [(t, half) for t in range(2) for half in range(2)]
            for sub in range(n_sub):
                rows = slice(sub * blk, (sub + 1) * blk)
                keys = slice(sub * blk, (sub + 2) * blk)
                qm = [jnp.where(low_half if half == 0 else jnp.logical_not(low_half),
                                q_ref[rows, (qslab0 + t) * LANES:(qslab0 + t + 1) * LANES] * scale, 0.0)
                      for t, half in heads]
                s = lax.dot_general(jnp.concatenate(qm, axis=0).astype(BF16), k_dup[keys],
                                    (((1,), (1,)), ((), ())), preferred_element_type=F32)
                s_prev = s[:, :blk]
                if sub == 0:
                    s_prev = jnp.where(first, -jnp.inf, s_prev)
                sf = jnp.where(from_prev, s_prev, s[:, blk:])
                sink = jnp.zeros((group * blk, 1), F32)
                for n, (t, half) in enumerate(heads):
                    sink = jnp.where(part == n, sink_ref[2 * (qslab0 + t) + half], sink)
                mx = jnp.maximum(jnp.max(sf, axis=-1, keepdims=True), sink)
                p = jnp.exp(sf - mx)
                denom = jnp.sum(p, axis=-1, keepdims=True) + jnp.exp(sink - mx)
                p2 = jnp.concatenate([jnp.where(from_prev, p, 0.0), jnp.where(from_prev, 0.0, p)], axis=1)
                pv = jnp.dot(p2.astype(BF16), v_dup[keys], preferred_element_type=F32) / denom
                for t in range(2):
                    lo = pv[(2 * t) * blk:(2 * t + 1) * blk]
                    hi = pv[(2 * t + 1) * blk:(2 * t + 2) * blk]
                    o_ref[rows, (qslab0 + t) * LANES:(qslab0 + t + 1) * LANES] = jnp.where(
                        low_half, lo, hi).astype(o_ref.dtype)


def _swa(qkv, sinks, *, batch, n_sub):
    m = qkv.shape[0]
    blk = SWA_BLOCK
    rows = n_sub * blk
    nt = m // batch // rows
    hq = SWA_HEADS * SWA_HD
    hkv = SWA_KV_HEADS * SWA_HD
    cur = lambda col: (lambda b, n: (b * nt + n, col))
    prev = lambda col: (lambda b, n: ((b * nt + n) * n_sub - jnp.minimum(n, 1), col))
    return pl.pallas_call(
        functools.partial(_swa_body, n_sub=n_sub),
        grid=(batch, nt),
        in_specs=[pl.BlockSpec(memory_space=pltpu.SMEM),
                  pl.BlockSpec((rows, hq), cur(0)),
                  pl.BlockSpec((blk, hkv), prev(hq // hkv)),
                  pl.BlockSpec((rows, hkv), cur(hq // hkv)),
                  pl.BlockSpec((blk, hkv), prev(hq // hkv + 1)),
                  pl.BlockSpec((rows, hkv), cur(hq // hkv + 1))],
        out_specs=pl.BlockSpec((rows, hq), cur(0)),
        out_shape=jax.ShapeDtypeStruct((m, hq), BF16),
        compiler_params=_cparams(("arbitrary", "arbitrary")),
        name="swa",
    )(sinks, qkv, qkv, qkv, qkv, qkv)


def _pair_lanes(vec_a, vec_b):
    n_pairs = vec_a.shape[0] // 2
    quad = jnp.concatenate([vec_a.reshape(n_pairs, 2), vec_b.reshape(n_pairs, 2)], axis=1)
    return jnp.pad(quad.reshape(1, 4 * n_pairs), ((0, 0), (0, LANES - 4 * n_pairs)))


def _pair_cols(w_a, w_b):
    d, h = w_a.shape
    quad = jnp.concatenate([w_a.reshape(d, h // 2, 2), w_b.reshape(d, h // 2, 2)], axis=2)
    return jnp.pad(quad.reshape(d, 2 * h), ((0, 0), (0, LANES - 2 * h)))


def kernel(x, ffn1_norm, ffn1_w_gu, ffn1_w_down, mix_norm, ffn2_norm, ffn2_w_gu, ffn2_w_down,
           a_w_in, a_w_conv, a_A_log, a_dt_bias, a_out_norm, a_w_out,
           b_w_in, b_b_in, b_sinks, b_w_out, b_b_out, final_norm):
    batch, seq, d = x.shape
    depth = ffn1_norm.shape[0]
    m = batch * seq
    tm = min(512, seq)
    tm_ffn = min(1024, seq)
    xf = x.reshape(m, d)
    row = lambda v: v.reshape(1, -1).astype(F32)

    for layer in range(depth):
        j = layer // 2
        last = layer == depth - 1
        ffn1 = (row(ffn1_norm[layer]), ffn1_w_gu, ffn1_w_down, layer)
        ffn2 = (row(ffn2_norm[layer]), ffn2_w_gu, ffn2_w_down, layer)
        if layer % 2 == 0:
            xf = _ffn(xf, *ffn1, tm=tm_ffn)
            hk = GDN_HEADS * GDN_DK
            w_b = a_w_in[j, :, 4 * hk:4 * hk + GDN_HEADS]
            w_a = a_w_in[j, :, 4 * hk + GDN_HEADS:]
            zeros_h = jnp.zeros((GDN_HEADS,), F32)
            qkvz, aux = _gdn_in(xf, row(mix_norm[layer]), jnp.swapaxes(a_w_in, 1, 2), j, _pair_cols(w_a, w_b),
                                a_w_conv[j].astype(F32), _pair_lanes(a_A_log[j].astype(F32), zeros_h),
                                _pair_lanes(a_dt_bias[j].astype(F32), zeros_h), batch=batch, tm=tm)
            o = _gdn_chunks(qkvz.reshape(batch, seq, -1), aux.reshape(batch, seq, -1),
                            jnp.tile(row(a_out_norm[j]), (1, 2)), n_chunks=min(8, seq // GDN_CHUNK))
            pre = (o.reshape(m, -1), a_w_out, j, jnp.zeros((1, d), F32))
        else:
            xf, qkv = _ffn(xf, *ffn1, tm=tm_ffn, proj=(row(mix_norm[layer]), b_w_in, j, row(b_b_in[j])))
            o = _swa(qkv, b_sinks[j].astype(F32), batch=batch, n_sub=min(8, seq // SWA_BLOCK))
            pre = (o, b_w_out, j, row(b_b_out[j]))
        xf = _ffn(xf, *ffn2, tm=tm_ffn, pre=pre, final_w=row(final_norm) if last else None)
    return xf.reshape(batch, seq, d)
```

```python
import functools

import jax
import jax.numpy as jnp
from jax import lax
from jax.experimental import pallas as pl
from jax.experimental.pallas import tpu as pltpu

F32 = jnp.float32
BF16 = jnp.bfloat16
EPS = 1e-6

LANES = 128
VMEM_LIMIT_BYTES = 62 * 2**20

GDN_HEADS = 8
GDN_DK = 128
GDN_CONV = 4
GDN_CHUNK = 128
SWA_HEADS = 16
SWA_KV_HEADS = 4
SWA_HD = 64
SWA_BLOCK = 128


def _cparams(semantics):
    return pltpu.CompilerParams(dimension_semantics=semantics, vmem_limit_bytes=VMEM_LIMIT_BYTES)


def _resident(shape):
    return pl.BlockSpec(shape, lambda *_: (0,) * len(shape), pipeline_mode=pl.Buffered(1))


def _rms(x, w):
    ms = jnp.mean(x * x, axis=-1, keepdims=True)
    return x * lax.rsqrt(ms + EPS) * w


def _silu(x):
    return x / (1.0 + jnp.exp(-x))


def _mm(a, b):
    return jnp.dot(a.astype(BF16), b.astype(BF16), preferred_element_type=F32)


N_WCHUNK = 16


def _w_spec(w, layer, n_rows=None):
    return pl.BlockSpec((None, (n_rows or w.shape[1]) // N_WCHUNK, w.shape[2]),
                        lambda i: (layer, jnp.minimum(i, N_WCHUNK - 1), 0))


def _w_scratch(w, n_rows=None):
    return pltpu.VMEM((n_rows or w.shape[1], w.shape[2]), BF16)


def _cast_in(i, pairs):
    @pl.when(i < N_WCHUNK)
    def _():
        for src, dst in pairs:
            rows = src.shape[0]
            dst[pl.ds(pl.multiple_of(i * rows, rows), rows), :] = src[:, :dst.shape[1]].astype(BF16)


def _tile_index(i):
    return jnp.maximum(i - N_WCHUNK, 0)


def _ffn_body(*refs, d_ff, has_pre, tail):
    i = pl.program_id(0)
    refs = list(refs)
    n_w = 2 + has_pre + (tail == "proj")
    w_s = refs[-n_w:]
    refs = refs[:-n_w]
    x_ref = refs.pop(0)
    casts = []
    if has_pre:
        o_ref, wo_ref, bo_ref = refs.pop(0), refs.pop(0), refs.pop(0)
        wo_s = w_s.pop(0)
        casts.append((wo_ref, wo_s))
    nw_ref, wgu_ref, wd_ref = refs.pop(0), refs.pop(0), refs.pop(0)
    wgu_s, wd_s = w_s.pop(0), w_s.pop(0)
    casts += [(wgu_ref, wgu_s), (wd_ref, wd_s)]
    if tail == "proj":
        nw2_ref, wp_ref, bp_ref, y_ref, p_ref = refs
        wp_s = w_s.pop(0)
        casts.append((wp_ref, wp_s))
    elif tail == "norm":
        fw_ref, y_ref = refs
    else:
        (y_ref,) = refs
    _cast_in(i, casts)

    @pl.when(i >= N_WCHUNK)
    def _():
        x = x_ref[...]
        if has_pre:
            x = x + jnp.dot(o_ref[...], wo_s[...], preferred_element_type=F32) + bo_ref[...]
        xn = _rms(x, nw_ref[...]).astype(BF16)
        gu = jnp.dot(xn, wgu_s[...], preferred_element_type=F32)
        h = (_silu(gu[:, :d_ff]) * gu[:, d_ff:]).astype(BF16)
        out = x + 0.5 * jnp.dot(h, wd_s[...], preferred_element_type=F32)
        if tail == "norm":
            y_ref[...] = _rms(out, fw_ref[...])
        else:
            y_ref[...] = out
        if tail == "proj":
            hn = _rms(out, nw2_ref[...]).astype(BF16)
            p_ref[...] = (jnp.dot(hn, wp_s[...], preferred_element_type=F32) + bp_ref[...]).astype(p_ref.dtype)


def _ffn(x, norm_w, w_gu, w_down, layer, *, tm, pre=None, final_w=None, proj=None):
    m, d = x.shape
    d_ff = w_down.shape[1]
    row = lambda n: pl.BlockSpec((tm, n), lambda i: (_tile_index(i), 0))
    args, specs, scratch = [x], [row(d)], []
    if pre is not None:
        o, w_o, j_o, b_o = pre
        args += [o, w_o, b_o]
        specs += [row(o.shape[1]), _w_spec(w_o, j_o), _resident((1, d))]
        scratch.append(_w_scratch(w_o))
    args += [norm_w, w_gu, w_down]
    specs += [_resident((1, d)), _w_spec(w_gu, layer), _w_spec(w_down, layer)]
    scratch += [_w_scratch(w_gu), _w_scratch(w_down)]
    out_specs, out_shape = row(d), jax.ShapeDtypeStruct((m, d), F32)
    tail = "plain"
    if final_w is not None:
        tail = "norm"
        args.append(final_w)
        specs.append(_resident((1, d)))
    elif proj is not None:
        tail = "proj"
        nw2, w_p, j_p, b_p = proj
        n_p = w_p.shape[2]
        args += [nw2, w_p, b_p]
        specs += [_resident((1, d)), _w_spec(w_p, j_p), _resident((1, n_p))]
        scratch.append(_w_scratch(w_p))
        out_specs = [row(d), row(n_p)]
        out_shape = [out_shape, jax.ShapeDtypeStruct((m, n_p), BF16)]
    return pl.pallas_call(
        functools.partial(_ffn_body, d_ff=d_ff, has_pre=pre is not None, tail=tail),
        grid=(N_WCHUNK + m // tm,),
        in_specs=specs,
        out_specs=out_specs,
        out_shape=out_shape,
        scratch_shapes=scratch,
        compiler_params=_cparams(("arbitrary",)),
        name="ffn",
    )(*args)


def _chunk_cumsum(g, chunk):
    pos = lax.broadcasted_iota(jnp.int32, g.shape, 0) % chunk
    shift = 1
    while shift < chunk:
        g = g + jnp.where(pos >= shift, pltpu.roll(g, shift, axis=0), 0.0)
        shift *= 2
    return g


def _softplus(x):
    return jnp.maximum(x, 0.0) + jnp.log(1.0 + jnp.exp(-jnp.abs(x)))


def _gdn_in_body(x_ref, nw_ref, wm_ref, wba_ref, wc_ref, alog_ref, dtb_ref,
                 qkvz_ref, aux_ref, carry, wm_s, *, tm, n_heads, dk, chunk, cb, tiles_per_seq):
    i = pl.program_id(0)
    _cast_in(i, [(wm_ref, wm_s)])

    @pl.when(i >= N_WCHUNK)
    def _():
        _gdn_in_tile(x_ref, nw_ref, wm_s, wba_ref, wc_ref, alog_ref, dtb_ref, qkvz_ref, aux_ref, carry,
                     first=(i - N_WCHUNK) % tiles_per_seq == 0, tm=tm, n_heads=n_heads, dk=dk,
                     chunk=chunk, cb=cb)


def _gdn_in_tile(x_ref, nw_ref, wm_ref, wba_ref, wc_ref, alog_ref, dtb_ref,
                 qkvz_ref, aux_ref, carry, *, first, tm, n_heads, dk, chunk, cb):
    hk = n_heads * dk
    halo = carry.shape[0]
    xn = _rms(x_ref[...], nw_ref[...]).astype(BF16)
    proj = lambda cols: lax.dot_general(xn, wm_ref[cols, :], (((1,), (1,)), ((), ())),
                                        preferred_element_type=F32)

    for j in range(3 * hk // cb):
        cols = slice(j * cb, (j + 1) * cb)
        pre = proj(cols)
        ext = jnp.concatenate([jnp.where(first, 0.0, carry[:, cols]), pre], axis=0)
        carry[:, cols] = pre[tm - halo:, :]
        wc = wc_ref[:, cols]
        ext1 = pltpu.roll(ext, 1, axis=0)
        lo = wc[1:2, :] * ext + wc[0:1, :] * ext1
        hi = wc[3:4, :] * ext + wc[2:3, :] * ext1
        acc = (hi + pltpu.roll(lo, 2, axis=0))[halo:, :]
        c = _silu(acc)
        if j * cb < 2 * hk:
            gain = dk ** -0.5 if j * cb < hk else 1.0
            for h in range(cb // dk):
                cs = c[:, h * dk:(h + 1) * dk]
                cn = cs * (lax.rsqrt(jnp.sum(cs * cs, axis=-1, keepdims=True) + EPS) * gain)
                qkvz_ref[:, j * cb + h * dk:j * cb + (h + 1) * dk] = cn.astype(BF16)
        else:
            qkvz_ref[:, cols] = c.astype(BF16)
    for j in range(3 * hk // cb, 4 * hk // cb):
        cols = slice(j * cb, (j + 1) * cb)
        qkvz_ref[:, cols] = proj(cols).astype(BF16)

    ba = jnp.dot(xn, wba_ref[...].astype(BF16), preferred_element_type=F32)
    g = -jnp.exp(alog_ref[...]) * _softplus(ba + dtb_ref[...])
    gc = _chunk_cumsum(g, chunk)
    beta = 1.0 / (1.0 + jnp.exp(-ba))
    lane = lax.broadcasted_iota(jnp.int32, ba.shape, 1)
    packed = jnp.where(lane % 4 < 2, gc, beta)
    for p in range(n_heads // 2):
        aux_ref[:, p * LANES:(p + 1) * LANES] = packed if p == 0 else pltpu.roll(packed, LANES - 4 * p, axis=1)


def _gdn_in(x, norm_w, w_in_t, layer, w_ba, w_conv, alog, dtb, *, batch, tm):
    m, d = x.shape
    nt = m // batch // tm
    hk = GDN_HEADS * GDN_DK
    n_main = 4 * hk
    n_aux = GDN_HEADS // 2 * LANES
    row = lambda i: (_tile_index(i), 0)
    return pl.pallas_call(
        functools.partial(_gdn_in_body, tm=tm, n_heads=GDN_HEADS, dk=GDN_DK, chunk=GDN_CHUNK, cb=512,
                          tiles_per_seq=nt),
        grid=(N_WCHUNK + batch * nt,),
        in_specs=[pl.BlockSpec((tm, d), row), _resident((1, d)), _w_spec(w_in_t, layer, n_main),
                  _resident((d, LANES)), _resident((GDN_CONV, 3 * hk)), _resident((1, LANES)),
                  _resident((1, LANES))],
        out_specs=[pl.BlockSpec((tm, n_main), row), pl.BlockSpec((tm, n_aux), row)],
        out_shape=[jax.ShapeDtypeStruct((m, n_main), BF16), jax.ShapeDtypeStruct((m, n_aux), F32)],
        scratch_shapes=[pltpu.VMEM((8, 3 * hk), F32), _w_scratch(w_in_t, n_main)],
        compiler_params=_cparams(("arbitrary",)),
        name="gdn_in",
    )(x, norm_w, w_in_t, w_ba, w_conv, alog, dtb)


def _bd(y, dk):
    zero = jnp.zeros((y.shape[0], dk), y.dtype)
    top = jnp.concatenate([y[:, :dk], zero], axis=1)
    bot = jnp.concatenate([zero, y[:, dk:]], axis=1)
    return jnp.concatenate([top, bot], axis=0)


def _pair_bcast(col_a, col_b, rows, dk):
    return jnp.concatenate([jnp.broadcast_to(col_a, (rows, dk)),
                            jnp.broadcast_to(col_b, (rows, dk))], axis=1)


def _gdn_chunk_body(q_ref, k_ref, v_ref, z_ref, aux_ref, onw_ref, o_ref, s_ref, *, n_chunks, dk):
    c_len = GDN_CHUNK
    w2 = 2 * dk
    n_batch = q_ref.shape[0]
    units = [(b, c) for b in range(n_batch) for c in range(n_chunks)]

    @pl.when(pl.program_id(1) == 0)
    def _():
        s_ref[...] = jnp.zeros_like(s_ref)

    row = lax.broadcasted_iota(jnp.int32, (c_len, w2), 0)
    col = lax.broadcasted_iota(jnp.int32, (c_len, w2), 1) % dk
    lower = row >= col
    strict = row > col
    eye = jnp.where(row == col, 1.0, 0.0)
    srow = lax.broadcasted_iota(jnp.int32, (w2, w2), 0) // dk
    scol = lax.broadcasted_iota(jnp.int32, (w2, w2), 1) // dk
    same_head = srow == scol
    bd = lambda y: _bd(y.astype(BF16), dk)
    rows_of = lambda c: slice(c * c_len, (c + 1) * c_len)

    q, k, kb, vb, gc, gc_last, e_gc, decay = [], [], [], [], [], [], [], []
    for b, c in units:
        aux = aux_ref[b, rows_of(c), :]
        g = _pair_bcast(aux[:, 0:1], aux[:, 1:2], c_len, dk)
        beta = _pair_bcast(aux[:, 2:3], aux[:, 3:4], c_len, dk)
        g_t = jnp.concatenate([g[:, :dk].T, g[:, dk:].T], axis=1)
        decay.append(jnp.where(lower, jnp.exp(jnp.minimum(g - g_t, 0.0)), 0.0))
        gc.append(g)
        gc_last.append(g[c_len - 1:c_len, :])
        e_gc.append(jnp.exp(g))
        q.append(q_ref[b, rows_of(c), :].astype(F32))
        k.append(k_ref[b, rows_of(c), :].astype(F32))
        kb.append(k[-1] * beta)
        vb.append(v_ref[b, rows_of(c), :].astype(F32) * beta)

    a_qk, lmat = [], []
    for i in range(len(units)):
        qk_kbk = lax.dot_general(jnp.concatenate([q[i], kb[i]], axis=0).astype(BF16), bd(k[i]),
                                 (((1,), (1,)), ((), ())), preferred_element_type=F32)
        a_qk.append(jnp.where(lower, qk_kbk[:c_len] * decay[i], 0.0))
        lmat.append(jnp.where(strict, qk_kbk[c_len:] * decay[i], 0.0))

    lmat = [l.astype(BF16) for l in lmat]
    zero = jnp.zeros((), BF16)
    p1 = [jnp.where(row // 8 == col // 8, l, zero) for l in lmat]
    p2 = [_mm(p, bd(p)).astype(BF16) for p in p1]
    p4 = [_mm(p, bd(p)).astype(BF16) for p in p2]
    inv = [(eye - p).astype(BF16) for p in p1]
    inv = [(x + _mm(x, bd(p))).astype(BF16) for x, p in zip(inv, p2)]
    inv = [(x + _mm(x, bd(p))).astype(BF16) for x, p in zip(inv, p4)]
    size = 8
    while size < c_len:
        sel = (row // (2 * size) == col // (2 * size)) & (row // size != col // size)
        t = [_mm(jnp.where(sel, l, zero), bd(x)).astype(BF16) for l, x in zip(lmat, inv)]
        inv = [(x - _mm(x, bd(y))).astype(BF16) for x, y in zip(inv, t)]
        size *= 2

    u = [_mm(x, bd(y)) for x, y in zip(inv, vb)]
    w = [_mm(x, bd(kb[i] * e_gc[i])) for i, x in enumerate(inv)]
    wq = [jnp.concatenate([w[i], q[i] * e_gc[i]], axis=0).astype(BF16) for i in range(len(units))]
    k_dec = [(k[i] * jnp.exp(gc_last[i] - gc[i])).astype(BF16) for i in range(len(units))]

    state = [s_ref[b] for b in range(n_batch)]
    for c in range(n_chunks):
        idx = [b * n_chunks + c for b in range(n_batch)]
        wq_s = [_mm(wq[i], state[b]) for b, i in enumerate(idx)]
        v_new = [u[i] - wq_s[b][:c_len] for b, i in enumerate(idx)]
        kv = [lax.dot_general(k_dec[i], v_new[b].astype(BF16), (((0,), (0,)), ((), ())),
                              preferred_element_type=F32) for b, i in enumerate(idx)]
        state = [state[b] * jnp.exp(gc_last[i]) + jnp.where(same_head, kv[b], 0.0)
                 for b, i in enumerate(idx)]
        for b, i in enumerate(idx):
            o = wq_s[b][c_len:] + _mm(a_qk[i], bd(v_new[b]))
            ms = _pair_bcast(jnp.mean(o[:, :dk] * o[:, :dk], axis=-1, keepdims=True),
                             jnp.mean(o[:, dk:] * o[:, dk:], axis=-1, keepdims=True), c_len, dk)
            z = z_ref[b, rows_of(c), :].astype(F32)
            o_ref[b, rows_of(c), :] = (o * lax.rsqrt(ms + EPS) * onw_ref[...] * _silu(z)).astype(o_ref.dtype)

    for b in range(n_batch):
        s_ref[b] = state[b]


def _gdn_chunks(qkvz, aux, out_norm2, *, n_chunks):
    batch, seq, _ = qkvz.shape
    dk = GDN_DK
    tb = n_chunks * GDN_CHUNK
    n_pairs = GDN_HEADS // 2
    spec = lambda offset: pl.BlockSpec((batch, tb, 2 * dk), lambda p, j: (0, j, offset + p))
    return pl.pallas_call(
        functools.partial(_gdn_chunk_body, n_chunks=n_chunks, dk=dk),
        grid=(n_pairs, seq // tb),
        in_specs=[spec(0), spec(n_pairs), spec(2 * n_pairs), spec(3 * n_pairs),
                  pl.BlockSpec((batch, tb, LANES), lambda p, j: (0, j, p)), _resident((1, 2 * dk))],
        out_specs=spec(0),
        out_shape=jax.ShapeDtypeStruct((batch, seq, GDN_HEADS * dk), BF16),
        scratch_shapes=[pltpu.VMEM((batch, 2 * dk, 2 * dk), F32)],
        compiler_params=_cparams(("arbitrary", "arbitrary")),
        name="gdn_chunks",
    )(qkvz, qkvz, qkvz, qkvz, aux, out_norm2)


def _swap_halves(x):
    half = x.shape[1] // 2
    return jnp.concatenate([x[:, half:], x[:, :half]], axis=1)


def _swa_body(sink_ref, q_ref, kp_ref, kc_ref, vp_ref, vc_ref, o_ref, *, n_sub):
    blk = SWA_BLOCK
    group = SWA_HEADS // SWA_KV_HEADS
    first = pl.program_id(1) == 0
    qi = lax.broadcasted_iota(jnp.int32, (group * blk, blk), 0) % blk
    kj = lax.broadcasted_iota(jnp.int32, (group * blk, blk), 1)
    from_prev = kj > qi
    part = lax.broadcasted_iota(jnp.int32, (group * blk, 1), 0) // blk
    lane = lax.broadcasted_iota(jnp.int32, (blk, LANES), 1)
    low_half = lane < SWA_HD
    scale = SWA_HD ** -0.5

    for slab in range(SWA_KV_HEADS // 2):
        cols = slice(slab * LANES, (slab + 1) * LANES)
        k_all = jnp.concatenate([kp_ref[:, cols], kc_ref[:, cols]], axis=0)
        v_all = jnp.concatenate([vp_ref[:, cols], vc_ref[:, cols]], axis=0)
        k_sw = _swap_halves(k_all)
        v_sw = _swap_halves(v_all)
        low_keys = lax.broadcasted_iota(jnp.int32, k_all.shape, 1) < SWA_HD
        for kv_half in range(2):
            k_dup = jnp.where(low_keys, k_all, k_sw) if kv_half == 0 else jnp.where(low_keys, k_sw, k_all)
            v_dup = jnp.where(low_keys, v_all, v_sw) if kv_half == 0 else jnp.where(low_keys, v_sw, v_all)
            qslab0 = 2 * (2 * slab + kv_half)
            heads = [(t, half) for t in range(2) for half in range(2)]
            for sub in range(n_sub):
                rows = slice(sub * blk, (sub + 1) * blk)
                keys = slice(sub * blk, (sub + 2) * blk)
                qm = [jnp.where(low_half if half == 0 else jnp.logical_not(low_half),
                                q_ref[rows, (qslab0 + t) * LANES:(qslab0 + t + 1) * LANES] * scale, 0.0)
                      for t, half in heads]
                s = lax.dot_general(jnp.concatenate(qm, axis=0).astype(BF16), k_dup[keys],
                                    (((1,), (1,)), ((), ())), preferred_element_type=F32)
                s_prev = s[:, :blk]
                if sub == 0:
                    s_prev = jnp.where(first, -jnp.inf, s_prev)
                sf = jnp.where(from_prev, s_prev, s[:, blk:])
                sink = jnp.zeros((group * blk, 1), F32)
                for n, (t, half) in enumerate(heads):
                    sink = jnp.where(part == n, sink_ref[2 * (qslab0 + t) + half], sink)
                mx = jnp.maximum(jnp.max(sf, axis=-1, keepdims=True), sink)
                p = jnp.exp(sf - mx)
                denom = jnp.sum(p, axis=-1, keepdims=True) + jnp.exp(sink - mx)
                p2 = jnp.concatenate([jnp.where(from_prev, p, 0.0), jnp.where(from_prev, 0.0, p)], axis=1)
                pv = jnp.dot(p2.astype(BF16), v_dup[keys], preferred_element_type=F32) / denom
                for t in range(2):
                    lo = pv[(2 * t) * blk:(2 * t + 1) * blk]
                    hi = pv[(2 * t + 1) * blk:(2 * t + 2) * blk]
                    o_ref[rows, (qslab0 + t) * LANES:(qslab0 + t + 1) * LANES] = jnp.where(
                        low_half, lo, hi).astype(o_ref.dtype)


def _swa(qkv, sinks, *, batch, n_sub):
    m = qkv.shape[0]
    blk = SWA_BLOCK
    rows = n_sub * blk
    nt = m // batch // rows
    hq = SWA_HEADS * SWA_HD
    hkv = SWA_KV_HEADS * SWA_HD
    cur = lambda col: (lambda b, n: (b * nt + n, col))
    prev = lambda col: (lambda b, n: ((b * nt + n) * n_sub - jnp.minimum(n, 1), col))
    return pl.pallas_call(
        functools.partial(_swa_body, n_sub=n_sub),
        grid=(batch, nt),
        in_specs=[pl.BlockSpec(memory_space=pltpu.SMEM),
                  pl.BlockSpec((rows, hq), cur(0)),
                  pl.BlockSpec((blk, hkv), prev(hq // hkv)),
                  pl.BlockSpec((rows, hkv), cur(hq // hkv)),
                  pl.BlockSpec((blk, hkv), prev(hq // hkv + 1)),
                  pl.BlockSpec((rows, hkv), cur(hq // hkv + 1))],
        out_specs=pl.BlockSpec((rows, hq), cur(0)),
        out_shape=jax.ShapeDtypeStruct((m, hq), BF16),
        compiler_params=_cparams(("arbitrary", "arbitrary")),
        name="swa",
    )(sinks, qkv, qkv, qkv, qkv, qkv)


def _pair_lanes(vec_a, vec_b):
    n_pairs = vec_a.shape[0] // 2
    quad = jnp.concatenate([vec_a.reshape(n_pairs, 2), vec_b.reshape(n_pairs, 2)], axis=1)
    return jnp.pad(quad.reshape(1, 4 * n_pairs), ((0, 0), (0, LANES - 4 * n_pairs)))


def _pair_cols(w_a, w_b):
    d, h = w_a.shape
    quad = jnp.concatenate([w_a.reshape(d, h // 2, 2), w_b.reshape(d, h // 2, 2)], axis=2)
    return jnp.pad(quad.reshape(d, 2 * h), ((0, 0), (0, LANES - 2 * h)))


def kernel(x, ffn1_norm, ffn1_w_gu, ffn1_w_down, mix_norm, ffn2_norm, ffn2_w_gu, ffn2_w_down,
           a_w_in, a_w_conv, a_A_log, a_dt_bias, a_out_norm, a_w_out,
           b_w_in, b_b_in, b_sinks, b_w_out, b_b_out, final_norm):
    batch, seq, d = x.shape
    depth = ffn1_norm.shape[0]
    m = batch * seq
    tm = min(1024, seq)
    tm_ffn = min(1024, seq)
    xf = x.reshape(m, d)
    row = lambda v: v.reshape(1, -1).astype(F32)

    for layer in range(depth):
        j = layer // 2
        last = layer == depth - 1
        ffn1 = (row(ffn1_norm[layer]), ffn1_w_gu, ffn1_w_down, layer)
        ffn2 = (row(ffn2_norm[layer]), ffn2_w_gu, ffn2_w_down, layer)
        if layer % 2 == 0:
            xf = _ffn(xf, *ffn1, tm=tm_ffn)
            hk = GDN_HEADS * GDN_DK
            w_b = a_w_in[j, :, 4 * hk:4 * hk + GDN_HEADS]
            w_a = a_w_in[j, :, 4 * hk + GDN_HEADS:]
            zeros_h = jnp.zeros((GDN_HEADS,), F32)
            qkvz, aux = _gdn_in(xf, row(mix_norm[layer]), jnp.swapaxes(a_w_in, 1, 2), j, _pair_cols(w_a, w_b),
                                a_w_conv[j].astype(F32), _pair_lanes(a_A_log[j].astype(F32), zeros_h),
                                _pair_lanes(a_dt_bias[j].astype(F32), zeros_h), batch=batch, tm=tm)
            o = _gdn_chunks(qkvz.reshape(batch, seq, -1), aux.reshape(batch, seq, -1),
                            jnp.tile(row(a_out_norm[j]), (1, 2)), n_chunks=min(16, seq // GDN_CHUNK))
            pre = (o.reshape(m, -1), a_w_out, j, jnp.zeros((1, d), F32))
        else:
            xf, qkv = _ffn(xf, *ffn1, tm=tm_ffn, proj=(row(mix_norm[layer]), b_w_in, j, row(b_b_in[j])))
            o = _swa(qkv, b_sinks[j].astype(F32), batch=batch, n_sub=min(16, seq // SWA_BLOCK))
            pre = (o, b_w_out, j, row(b_b_out[j]))
        xf = _ffn(xf, *ffn2, tm=tm_ffn, pre=pre, final_w=row(final_norm) if last else None)
    return xf.reshape(batch, seq, d)
```
